```python
import math
import jax, jax.numpy as jnp
from jax import lax
import numpy as np

D_MODEL = 1024
BATCH = 16
SEQ = 256
DEPTH = 4
DEC_BATCH = 4
DEC_SEQ = 4096
PAST_LEN = 512

GRID_W = 64
H_A = 4
DQK_A = 64
DV_A = 128
QK_W_A = H_A * 2 * DQK_A
W_A = H_A * DV_A
G_B = 4
DG_B = 64
W_B = G_B * DG_B
CHUNK_B = 128
H_C = 4
DK_C = 64
DV_C = 64
WK_C = H_C * DK_C
W_C = H_C * DV_C
CHUNK_C = 64
MIX_W = W_A + W_B + W_C
OFF_QA = 0
OFF_KA = OFF_QA + QK_W_A
OFF_VA = OFF_KA + QK_W_A
OFF_UB = OFF_VA + W_A
OFF_VB = OFF_UB + W_B
OFF_IC = OFF_VB + W_B
OFF_FC = OFF_IC + W_C
OFF_QC = OFF_FC + 2 * WK_C
OFF_GC = OFF_QC + 2 * WK_C
N_IN = OFF_GC + W_C
Q_BLOCK = 128
ROPE_THETA = 10000.0
N_EXP = 16
N_GROUP = 4
EXP_PER_GROUP = N_EXP // N_GROUP
TOP_K = 2
GROUP_TOP = 2
D_FF_E = 512
MOE_BLOCK = 128
LN_EPS = 1e-5
ALPHA = (2 * DEPTH) ** 0.25
BETA = (8 * DEPTH) ** -0.25

kernel_name = 'hybrid_diffusion_step'


def layer_norm(x, g, b):
    xf = x.astype(jnp.float32)
    mu = xf.mean(-1, keepdims=True)
    var = jnp.square(xf - mu).mean(-1, keepdims=True)
    y = (xf - mu) * lax.rsqrt(var + LN_EPS)
    return (y * g.astype(jnp.float32) + b.astype(jnp.float32)).astype(x.dtype)


def rms_norm(x, g):
    xf = x.astype(jnp.float32)
    y = xf * lax.rsqrt(jnp.mean(xf * xf, -1, keepdims=True) + LN_EPS)
    return (y * g.astype(jnp.float32)).astype(x.dtype)


def ada_modulation(cond, w, b):
    m = jax.nn.silu(cond) @ w + b
    return m.reshape(cond.shape[0], 6, D_MODEL)


def modulate(x, shift, scale):
    return x * (1 + scale[:, None, :]) + shift[:, None, :]


def axial_rope(x):
    L = x.shape[1]
    rows_n = L // GRID_W
    row = jnp.repeat(jnp.arange(rows_n), GRID_W).astype(jnp.float32)
    col = jnp.tile(jnp.arange(GRID_W), rows_n).astype(jnp.float32)
    half = DQK_A // 2
    quarter = half // 2
    inv = ROPE_THETA ** (-jnp.arange(quarter, dtype=jnp.float32) / quarter)

    def rot(xp, pos):
        ang = pos[:, None] * inv
        cos = jnp.cos(ang)[None, :, None, None, :]
        sin = jnp.sin(ang)[None, :, None, None, :]
        x1, x2 = xp[..., :quarter], xp[..., quarter:]
        return jnp.concatenate([x1 * cos - x2 * sin, x2 * cos + x1 * sin], -1)

    out = jnp.concatenate([rot(x[..., :half], row), rot(x[..., half:], col)], -1)
    return out.astype(x.dtype)


def attn_split(proj):
    B, L = proj.shape[:2]
    q = proj[..., OFF_QA:OFF_QA + QK_W_A].reshape(B, L, H_A, 2, DQK_A)
    k = proj[..., OFF_KA:OFF_KA + QK_W_A].reshape(B, L, H_A, 2, DQK_A)
    v = proj[..., OFF_VA:OFF_VA + W_A].reshape(B, L, H_A, DV_A)
    return q, k, v


def diff_attention(q, k, v, lam):
    B, Lq = q.shape[:2]
    nb = Lq // Q_BLOCK
    qb = q.reshape(B, nb, Q_BLOCK, H_A, 2, DQK_A).swapaxes(0, 1)
    scale = DQK_A ** -0.5

    def one(qblk):
        s = jnp.einsum('bqhmd,bkhmd->bhmqk', qblk, k).astype(jnp.float32) * scale
        p = jax.nn.softmax(s, axis=-1)
        pd = p[:, :, 0] - lam * p[:, :, 1]
        return jnp.einsum('bhqk,bkhe->bqhe', pd.astype(v.dtype), v)

    o = lax.map(one, qb)
    return o.swapaxes(0, 1).reshape(B, Lq, H_A, DV_A)


def diff_attn_out(o, g, lam_init):
    B, L = o.shape[:2]
    return (rms_norm(o, g) * (1.0 - lam_init)).reshape(B, L, W_A)


def chunk_mlp(proj, ln_g, ln_b, w_s, b_s):
    B, L = proj.shape[:2]
    u = jax.nn.gelu(proj[..., OFF_UB:OFF_UB + W_B])
    v = layer_norm(jax.nn.gelu(proj[..., OFF_VB:OFF_VB + W_B]), ln_g, ln_b)
    vr = v.reshape(B, L // CHUNK_B, CHUNK_B, G_B, DG_B)
    mixed = jnp.einsum('gij,bnjgd->bnigd', w_s, vr) + b_s.T[None, None, :, :, None]
    return u * mixed.reshape(B, L, W_B)


def hgrn_lower_bounds(lb_logits):
    p = jax.nn.softmax(lb_logits.astype(jnp.float32), axis=1)
    cs = jnp.cumsum(p, axis=1)
    return cs - cs[:, :1]


def hgrn_scan(q, k, logf, i, s0):
    B, L = q.shape[:2]
    n = L // CHUNK_C

    def chunks(t):
        return t.astype(jnp.float32).reshape(B, n, CHUNK_C, *t.shape[2:]).swapaxes(0, 1)

    mask = jnp.tril(jnp.ones((CHUNK_C, CHUNK_C), bool))[None, :, :, None, None]

    def step(S, inp):
        qc, kc, gc, ic = inp
        b = jnp.cumsum(gc, axis=1)
        o_inter = jnp.einsum('bthd,bhde->bthe', qc * jnp.exp(b), S)
        diff = b[:, :, None] - b[:, None, :]
        dec = jnp.exp(jnp.where(mask, diff, -jnp.inf))
        a = jnp.einsum('bthd,bshd,btshd->bhts', qc, kc, dec)
        o_intra = jnp.einsum('bhts,bshe->bthe', a, ic)
        b_last = b[:, -1]
        S_new = jnp.exp(b_last)[..., None] * S + jnp.einsum(
            'bshd,bshe->bhde', kc * jnp.exp(b_last[:, None] - b), ic)
        return S_new, o_inter + o_intra

    s_fin, o = lax.scan(step, s0.astype(jnp.float32), (chunks(q), chunks(k), chunks(logf), chunks(i)))
    return o.swapaxes(0, 1).reshape(B, L, H_C, DV_C), s_fin


def hgrn_mixer(proj, lb, norm_g, s_fwd0, s_bwd0):
    B, L = proj.shape[:2]
    i = proj[..., OFF_IC:OFF_IC + W_C].reshape(B, L, H_C, DV_C)
    gate = proj[..., OFF_GC:OFF_GC + W_C]
    outs = []
    states = []
    for d, s0 in enumerate((s_fwd0, s_bwd0)):
        fpre = proj[..., OFF_FC + d * WK_C:OFF_FC + (d + 1) * WK_C].astype(jnp.float32)
        q = proj[..., OFF_QC + d * WK_C:OFF_QC + (d + 1) * WK_C].reshape(B, L, H_C, DK_C)
        f = lb[d] + (1.0 - lb[d]) * jax.nn.sigmoid(fpre)
        logf = jnp.log(f).reshape(B, L, H_C, DK_C)
        k = (1.0 - f).reshape(B, L, H_C, DK_C)
        ii = i
        if d == 1:
            q, k, logf, ii = jnp.flip(q, 1), jnp.flip(k, 1), jnp.flip(logf, 1), jnp.flip(ii, 1)
        o, s = hgrn_scan(q, k, logf, ii, s0)
        if d == 1:
            o = jnp.flip(o, 1)
        outs.append(o)
        states.append(s)
    o = rms_norm(outs[0] + outs[1], norm_g.reshape(H_C, DV_C)).astype(proj.dtype)
    o = o.reshape(B, L, W_C) * jax.nn.silu(gate)
    return o, jnp.stack(states, axis=1)


def moe_ffn(h, w_router, router_bias, wg, wu, wd):
    B, L, D = h.shape
    T = B * L
    xt = h.reshape(T, D)
    scores = jax.nn.sigmoid((xt @ w_router).astype(jnp.float32))
    sel = scores + router_bias.astype(jnp.float32)
    gscore = lax.top_k(sel.reshape(T, N_GROUP, EXP_PER_GROUP), GROUP_TOP)[0].sum(-1)
    gbest = jnp.argmax(gscore, axis=-1)
    in_grp = (jnp.arange(N_EXP) // EXP_PER_GROUP)[None, :] == gbest[:, None]
    _, eidx = lax.top_k(jnp.where(in_grp, sel, -jnp.inf), TOP_K)
    wsel = jnp.take_along_axis(scores, eidx, axis=-1)
    wsel = wsel / wsel.sum(-1, keepdims=True)
    A = T * TOP_K
    e_flat = eidx.reshape(-1)
    t_flat = jnp.repeat(jnp.arange(T, dtype=jnp.int32), TOP_K)
    w_flat = wsel.reshape(-1)
    order = jnp.argsort(e_flat)
    e_s, t_s, w_s = e_flat[order], t_flat[order], w_flat[order]
    counts = jax.ops.segment_sum(jnp.ones((A,), jnp.int32), e_flat, num_segments=N_EXP)
    starts = jnp.cumsum(counts) - counts
    pcounts = (counts + MOE_BLOCK - 1) // MOE_BLOCK * MOE_BLOCK
    pends = jnp.cumsum(pcounts)
    pstarts = pends - pcounts
    dest = pstarts[e_s] + jnp.arange(A, dtype=jnp.int32) - starts[e_s]
    n_blocks = (A + MOE_BLOCK - 1) // MOE_BLOCK + N_EXP
    n_rows = n_blocks * MOE_BLOCK
    row_tok = jnp.zeros((n_rows,), jnp.int32).at[dest].set(t_s)
    row_w = jnp.zeros((n_rows,), jnp.float32).at[dest].set(w_s)
    block_e = jnp.minimum(
        jnp.searchsorted(pends, jnp.arange(n_blocks, dtype=jnp.int32) * MOE_BLOCK, side='right'),
        N_EXP - 1)
    xb = xt[row_tok].reshape(n_blocks, MOE_BLOCK, D)

    def expert_block(args):
        xblk, e = args
        a = jax.nn.silu(xblk @ wg[e]) * (xblk @ wu[e])
        return a @ wd[e]

    yb = lax.map(expert_block, (xb, block_e)).reshape(n_rows, D)
    y = jax.ops.segment_sum(yb * row_w[:, None].astype(yb.dtype), row_tok, num_segments=T)
    return y.reshape(B, L, D)


def residual_ffn(x, m, mixed, w_out, ln_g, ln_b, w_router, router_bias, wg, wu, wd):
    x = layer_norm(ALPHA * x + m[:, 2][:, None, :] * (mixed @ w_out), ln_g[0], ln_b[0])
    h = modulate(x, m[:, 3], m[:, 4])
    y = moe_ffn(h, w_router, router_bias, wg, wu, wd)
    return layer_norm(ALPHA * x + m[:, 5][:, None, :] * y, ln_g[1], ln_b[1])


def setup_inputs(seed: int = 0) -> dict:
    key = jax.random.key(seed)
    ks = jax.random.split(key, 26)
    f32 = jnp.float32
    nrm = lambda k, s: jax.random.normal(k, s, f32)
    return {
        'x_prompt': nrm(ks[0], (BATCH, SEQ, D_MODEL)),
        'x_sample': nrm(ks[1], (DEC_BATCH, DEC_SEQ, D_MODEL)),
        'c': nrm(ks[2], (DEC_BATCH, D_MODEL)),
        'cache_k': nrm(ks[3], (DEC_BATCH, DEPTH, PAST_LEN, H_A, 2, DQK_A)),
        'cache_v': nrm(ks[4], (DEC_BATCH, DEPTH, PAST_LEN, H_A, DV_A)),
        'state_hgrn': 0.5 * nrm(ks[5], (DEC_BATCH, DEPTH, 2, H_C, DK_C, DV_C)),
        'c_ctx': nrm(ks[6], (D_MODEL,)),
        'w_ada': 0.5 * D_MODEL ** -0.5 * nrm(ks[7], (DEPTH, D_MODEL, 6 * D_MODEL)),
        'b_ada': 0.02 * nrm(ks[8], (DEPTH, 6 * D_MODEL)),
        'w_in': D_MODEL ** -0.5 * nrm(ks[9], (DEPTH, D_MODEL, N_IN)),
        'w_out': BETA * MIX_W ** -0.5 * nrm(ks[10], (DEPTH, MIX_W, D_MODEL)),
        'diff_lambda': 0.1 * nrm(ks[11], (DEPTH, 4, DQK_A)),
        'attn_norm_g': 1.0 + 0.02 * nrm(ks[12], (DEPTH, DV_A)),
        'mlp_ln_g': 1.0 + 0.02 * nrm(ks[13], (DEPTH, W_B)),
        'mlp_ln_b': 0.02 * nrm(ks[14], (DEPTH, W_B)),
        'w_spatial': CHUNK_B ** -0.5 * nrm(ks[15], (DEPTH, G_B, CHUNK_B, CHUNK_B)),
        'b_spatial': 1.0 + 0.02 * nrm(ks[16], (DEPTH, G_B, CHUNK_B)),
        'hgrn_lb': nrm(ks[17], (2, DEPTH, WK_C)),
        'hgrn_norm_g': 1.0 + 0.02 * nrm(ks[18], (DEPTH, W_C)),
        'ln_g': 1.0 + 0.02 * nrm(ks[19], (DEPTH, 2, D_MODEL)),
        'ln_b': 0.02 * nrm(ks[20], (DEPTH, 2, D_MODEL)),
        'w_router': D_MODEL ** -0.5 * nrm(ks[21], (D_MODEL, N_EXP)),
        'router_bias': 0.01 * nrm(ks[22], (N_EXP,)),
        'w_gate': D_MODEL ** -0.5 * nrm(ks[23], (DEPTH, N_EXP, D_MODEL, D_FF_E)),
        'w_up': D_MODEL ** -0.5 * nrm(ks[24], (DEPTH, N_EXP, D_MODEL, D_FF_E)),
        'w_down': BETA * D_FF_E ** -0.5 * nrm(ks[25], (DEPTH, N_EXP, D_FF_E, D_MODEL)),
    }


def reference(x_prompt, x_sample, c, cache_k, cache_v, state_hgrn, c_ctx, w_ada, b_ada, w_in, w_out,
              diff_lambda, attn_norm_g, mlp_ln_g, mlp_ln_b, w_spatial, b_spatial, hgrn_lb, hgrn_norm_g,
              ln_g, ln_b, w_router, router_bias, w_gate, w_up, w_down):
    lb_all = hgrn_lower_bounds(hgrn_lb)
    xp, xs = x_prompt, x_sample
    zero_state = jnp.zeros((xp.shape[0], H_C, DK_C, DV_C), jnp.float32)
    new_k, new_v, new_s = [], [], []
    for l in range(DEPTH):
        lam_init = 0.8 - 0.6 * math.exp(-0.3 * l)
        dl = diff_lambda[l].astype(jnp.float32)
        lam = jnp.exp(jnp.sum(dl[0] * dl[1])) - jnp.exp(jnp.sum(dl[2] * dl[3])) + lam_init

        m_p = ada_modulation(c_ctx[None, :], w_ada[l], b_ada[l])
        h = modulate(xp, m_p[:, 0], m_p[:, 1])
        proj = h @ w_in[l]
        q, k, v = attn_split(proj)
        o_a = diff_attn_out(diff_attention(q, k, v, lam), attn_norm_g[l], lam_init)
        o_b = chunk_mlp(proj, mlp_ln_g[l], mlp_ln_b[l], w_spatial[l], b_spatial[l])
        o_c, s_ctx = hgrn_mixer(proj, lb_all[:, l], hgrn_norm_g[l], zero_state, zero_state)
        mixed = jnp.concatenate([o_a.astype(proj.dtype), o_b, o_c], axis=-1)
        xp = residual_ffn(xp, m_p, mixed, w_out[l], ln_g[l], ln_b[l], w_router, router_bias,
                          w_gate[l], w_up[l], w_down[l])
        new_k.append(k)
        new_v.append(v)
        new_s.append(s_ctx.astype(xp.dtype))

        m_s = ada_modulation(c, w_ada[l], b_ada[l])
        h = modulate(xs, m_s[:, 0], m_s[:, 1])
        proj = h @ w_in[l]
        q, k, v = attn_split(proj)
        q, k = axial_rope(q), axial_rope(k)
        k_all = jnp.concatenate([cache_k[:, l].astype(k.dtype), k], axis=1)
        v_all = jnp.concatenate([cache_v[:, l].astype(v.dtype), v], axis=1)
        o_a = diff_attn_out(diff_attention(q, k_all, v_all, lam), attn_norm_g[l], lam_init)
        o_b = chunk_mlp(proj, mlp_ln_g[l], mlp_ln_b[l], w_spatial[l], b_spatial[l])
        o_c, _ = hgrn_mixer(proj, lb_all[:, l], hgrn_norm_g[l], state_hgrn[:, l, 0], state_hgrn[:, l, 1])
        mixed = jnp.concatenate([o_a.astype(proj.dtype), o_b, o_c], axis=-1)
        xs = residual_ffn(xs, m_s, mixed, w_out[l], ln_g[l], ln_b[l], w_router, router_bias,
                          w_gate[l], w_up[l], w_down[l])
    return (xp, xs, jnp.stack(new_k, axis=1), jnp.stack(new_v, axis=1), jnp.stack(new_s, axis=1))
```

```python
import functools
import math

import jax
import jax.numpy as jnp
from jax import lax
from jax.experimental import pallas as pl
from jax.experimental.pallas import tpu as pltpu

D_MODEL = 1024
BATCH = 16
SEQ = 256
DEPTH = 4
DEC_BATCH = 4
DEC_SEQ = 4096
PAST_LEN = 512
GRID_W = 64
H_A = 4
DQK_A = 64
DV_A = 128
QK_W_A = H_A * 2 * DQK_A
W_A = H_A * DV_A
G_B = 4
DG_B = 64
W_B = G_B * DG_B
CHUNK_B = 128
H_C = 4
DK_C = 64
DV_C = 64
WK_C = H_C * DK_C
W_C = H_C * DV_C
MIX_W = W_A + W_B + W_C
OFF_QA = 0
OFF_KA = OFF_QA + QK_W_A
OFF_VA = OFF_KA + QK_W_A
OFF_UB = OFF_VA + W_A
OFF_VB = OFF_UB + W_B
OFF_IC = OFF_VB + W_B
OFF_FC = OFF_IC + W_C
OFF_QC = OFF_FC + 2 * WK_C
OFF_GC = OFF_QC + 2 * WK_C
N_IN = OFF_GC + W_C
ROPE_THETA = 10000.0
N_EXP = 16
N_GROUP = 4
EXP_PER_GROUP = N_EXP // N_GROUP
TOP_K = 2
GROUP_TOP = 2
D_FF_E = 512
LN_EPS = 1e-5
ALPHA = (2 * DEPTH) ** 0.25

T_CTX = BATCH * SEQ
T_DEC = DEC_BATCH * DEC_SEQ
T_ALL = T_CTX + T_DEC
N_SEQ = BATCH + DEC_BATCH
N_COND = 1 + DEC_BATCH
COND_PAD = 8

LANES = 128
TM = 256
CTX_TILES = T_CTX // TM
DEC_TILES_PER_SEQ = DEC_SEQ // TM
N_TILES = T_ALL // TM
HB = 16
MOE_ROWS = 128
N_PAIR = N_GROUP * (EXP_PER_GROUP * (EXP_PER_GROUP - 1) // 2)
MOE_BLOCKS = T_ALL // MOE_ROWS + N_PAIR
MOE_NROWS = MOE_BLOCKS * MOE_ROWS
VMEM_LIMIT = 56 * 1024 * 1024

BF16 = jnp.bfloat16
F32 = jnp.float32
HIGHEST = lax.Precision.HIGHEST
NT_DIMS = (((1,), (1,)), ((), ()))
TN_DIMS = (((0,), (0,)), ((), ()))


def _cparams(*sem):
    return pltpu.CompilerParams(dimension_semantics=sem, vmem_limit_bytes=VMEM_LIMIT)


def _tile_group(i):
    return jnp.where(i < CTX_TILES, 0, 1 + (i - CTX_TILES) // DEC_TILES_PER_SEQ)


def _layer_norm_rows(z, g, b):
    mu = jnp.mean(z, axis=-1, keepdims=True)
    zc = z - mu
    var = jnp.mean(zc * zc, axis=-1, keepdims=True)
    return zc * lax.rsqrt(var + LN_EPS) * g + b


ADA_TN = 1536


def _ada_kernel(cond_ref, w_ref, b_ref, o_ref):
    a = jax.nn.silu(cond_ref[...]).astype(BF16)
    o_ref[0] = jnp.dot(a, w_ref[0].astype(BF16), preferred_element_type=F32) + b_ref[0]


def _ada_modulation(cond, w_ada, b_ada):
    n = 6 * D_MODEL
    out = pl.pallas_call(
        _ada_kernel,
        grid=(DEPTH, n // ADA_TN),
        in_specs=[
            pl.BlockSpec((COND_PAD, D_MODEL), lambda l, j: (0, 0)),
            pl.BlockSpec((1, D_MODEL, ADA_TN), lambda l, j: (l, 0, j)),
            pl.BlockSpec((1, 1, ADA_TN), lambda l, j: (l, 0, j)),
        ],
        out_specs=pl.BlockSpec((1, COND_PAD, ADA_TN), lambda l, j: (l, 0, j)),
        out_shape=jax.ShapeDtypeStruct((DEPTH, COND_PAD, n), F32),
        compiler_params=_cparams("parallel", "parallel"),
        name="ada_modulation",
    )(cond, w_ada, b_ada.reshape(DEPTH, 1, n))
    return out.reshape(DEPTH, COND_PAD, 6, D_MODEL)


INPROJ_TN = 512


def _rope_tables():
    pos = jnp.arange(DEC_SEQ)
    row = (pos // GRID_W).astype(F32)
    col = (pos % GRID_W).astype(F32)
    quarter = DQK_A // 4
    inv = ROPE_THETA ** (-jnp.arange(quarter, dtype=F32) / quarter)
    lane = jnp.arange(LANES)
    in_head = lane % DQK_A
    use_col = in_head >= DQK_A // 2
    freq = inv[lane % quarter]
    ang = jnp.where(use_col[None, :], col[:, None], row[:, None]) * freq[None, :]
    sign = jnp.where((lane % (2 * quarter)) < quarter, -1.0, 1.0).astype(F32)
    cos = jnp.concatenate([jnp.cos(ang), jnp.ones((TM, LANES), F32)], axis=0)
    sin = jnp.concatenate([jnp.sin(ang) * sign[None, :], jnp.zeros((TM, LANES), F32)], axis=0)
    return cos, sin


def _inproj_kernel(x_ref, mod_ref, w_ref, cos_ref, sin_ref, o_ref):
    shift = mod_ref[0, 0, 0:1, :]
    scale = mod_ref[0, 0, 1:2, :]
    h = (x_ref[...] * (1.0 + scale) + shift).astype(BF16)
    cos = cos_ref[...]
    sin = sin_ref[...]
    lane = lax.broadcasted_iota(jnp.int32, (TM, LANES), 1)
    first = (lane % (DQK_A // 2)) < (DQK_A // 4)
    for j in range(N_IN // INPROJ_TN):
        acc = jnp.dot(h, w_ref[0, :, j * INPROJ_TN:(j + 1) * INPROJ_TN], preferred_element_type=F32)
        if (j + 1) * INPROJ_TN <= OFF_VA:
            for cb in range(INPROJ_TN // LANES):
                blk = acc[:, cb * LANES:(cb + 1) * LANES]
                partner = jnp.where(first, pltpu.roll(blk, LANES - DQK_A // 4, 1), pltpu.roll(blk, DQK_A // 4, 1))
                o_ref[:, j * INPROJ_TN + cb * LANES:j * INPROJ_TN + (cb + 1) * LANES] = blk * cos + partner * sin
        else:
            o_ref[:, j * INPROJ_TN:(j + 1) * INPROJ_TN] = acc


def _inproj(x, mods, w_in_bf, cos_t, sin_t, l):
    def rope_blk(i):
        return jnp.where(i < CTX_TILES, DEC_TILES_PER_SEQ, (i - CTX_TILES) % DEC_TILES_PER_SEQ)

    return pl.pallas_call(
        _inproj_kernel,
        grid=(N_TILES,),
        in_specs=[
            pl.BlockSpec((TM, D_MODEL), lambda i: (i, 0)),
            pl.BlockSpec((1, 1, 6, D_MODEL), lambda i: (l, _tile_group(i), 0, 0)),
            pl.BlockSpec((1, D_MODEL, N_IN), lambda i: (l, 0, 0)),
            pl.BlockSpec((TM, LANES), lambda i: (rope_blk(i), 0)),
            pl.BlockSpec((TM, LANES), lambda i: (rope_blk(i), 0)),
        ],
        out_specs=pl.BlockSpec((TM, N_IN), lambda i: (i, 0)),
        out_shape=jax.ShapeDtypeStruct((T_ALL, N_IN), F32),
        compiler_params=_cparams("arbitrary"),
        name="inproj",
    )(x, mods, w_in_bf, cos_t, sin_t)


def _attn_kernel(*refs, lam_init, has_cache):
    if has_cache:
        q_ref, kn_ref, vn_ref, kc_ref, vc_ref, dl_ref, g_ref, o_ref = refs
    else:
        q_ref, kn_ref, vn_ref, dl_ref, g_ref, o_ref = refs
    dl = dl_ref[0]
    lam = (jnp.exp(jnp.sum(dl[0:1] * dl[1:2], axis=-1, keepdims=True))
           - jnp.exp(jnp.sum(dl[2:3] * dl[3:4], axis=-1, keepdims=True)) + lam_init)
    q = q_ref[...] * (DQK_A ** -0.5 * math.log2(math.e))
    lane = lax.broadcasted_iota(jnp.int32, q.shape, 1)
    keys = [kn_ref[...].astype(BF16)]
    vals = [vn_ref[...].astype(BF16)]
    if has_cache:
        keys.append(kc_ref[0, 0].astype(BF16))
        vals.append(vc_ref[0, 0].astype(BF16))
    outs = []
    for m in range(2):
        qm = jnp.where((lane // DQK_A) == m, q, 0.0).astype(BF16)
        ss = [lax.dot_general(qm, k, NT_DIMS, preferred_element_type=F32) for k in keys]
        mx = functools.reduce(jnp.maximum, [jnp.max(s, axis=-1, keepdims=True) for s in ss])
        ps = [jnp.exp2(s - mx) for s in ss]
        den = functools.reduce(jnp.add, [jnp.sum(p, axis=-1, keepdims=True) for p in ps])
        acc = functools.reduce(jnp.add, [jnp.dot(p.astype(BF16), v, preferred_element_type=F32)
                                         for p, v in zip(ps, vals)])
        outs.append(acc / den)
    o = outs[0] - lam * outs[1]
    ms = jnp.mean(o * o, axis=-1, keepdims=True)
    o_ref[...] = o * lax.rsqrt(ms + LN_EPS) * g_ref[0] * (1.0 - lam_init)


def _attention(proj, cache_k4, cache_v4, diff_lambda, attn_norm_g, l, decode):
    lam_init = 0.8 - 0.6 * math.exp(-0.3 * l)
    qcol, kcol, vcol = OFF_QA // LANES, OFF_KA // LANES, OFF_VA // LANES
    common = [
        pl.BlockSpec((1, 4, DQK_A), lambda b, h, i: (l, 0, 0)),
        pl.BlockSpec((1, 1, DV_A), lambda b, h, i: (l, 0, 0)),
    ]
    if decode:
        seq_blk0 = T_CTX // DEC_SEQ
        grid = (DEC_BATCH, H_A, DEC_TILES_PER_SEQ)
        in_specs = [
            pl.BlockSpec((TM, LANES), lambda b, h, i: (CTX_TILES + b * DEC_TILES_PER_SEQ + i, qcol + h)),
            pl.BlockSpec((DEC_SEQ, LANES), lambda b, h, i: (seq_blk0 + b, kcol + h)),
            pl.BlockSpec((DEC_SEQ, LANES), lambda b, h, i: (seq_blk0 + b, vcol + h)),
            pl.BlockSpec((1, 1, PAST_LEN, LANES), lambda b, h, i: (b, l, 0, h)),
            pl.BlockSpec((1, 1, PAST_LEN, LANES), lambda b, h, i: (b, l, 0, h)),
        ] + common
        args = (proj, proj, proj, cache_k4, cache_v4)
        out_rows = T_DEC
        out_spec = pl.BlockSpec((TM, LANES), lambda b, h, i: (b * DEC_TILES_PER_SEQ + i, h))
    else:
        grid = (BATCH, H_A, 1)
        in_specs = [
            pl.BlockSpec((SEQ, LANES), lambda b, h, i: (b, qcol + h)),
            pl.BlockSpec((SEQ, LANES), lambda b, h, i: (b, kcol + h)),
            pl.BlockSpec((SEQ, LANES), lambda b, h, i: (b, vcol + h)),
        ] + common
        args = (proj, proj, proj)
        out_rows = T_CTX
        out_spec = pl.BlockSpec((SEQ, LANES), lambda b, h, i: (b, h))
    return pl.pallas_call(
        functools.partial(_attn_kernel, lam_init=lam_init, has_cache=decode),
        grid=grid,
        in_specs=in_specs,
        out_specs=out_spec,
        out_shape=jax.ShapeDtypeStruct((out_rows, W_A), F32),
        compiler_params=_cparams("parallel", "parallel", "arbitrary"),
        name="attn_dec" if decode else "attn_ctx",
    )(*args, diff_lambda, attn_norm_g.reshape(DEPTH, 1, DV_A))


GMLP_TM = 512


def _gmlp_kernel(u_ref, v_ref, g_ref, b_ref, ws_ref, bs_ref, o_ref):
    lane = lax.broadcasted_iota(jnp.int32, (CHUNK_B, W_B), 1)
    masks = [(lane // DG_B) == g for g in range(G_B)]
    bias = functools.reduce(
        jnp.add, [jnp.where(masks[g], bs_ref[0, :, g:g + 1], 0.0) for g in range(G_B)])
    for c in range(GMLP_TM // CHUNK_B):
        rows = slice(c * CHUNK_B, (c + 1) * CHUNK_B)
        u = jax.nn.gelu(u_ref[rows, :])
        v = _layer_norm_rows(jax.nn.gelu(v_ref[rows, :]), g_ref[0], b_ref[0])
        mixed = bias
        for g in range(G_B):
            vg = jnp.where(masks[g], v, 0.0).astype(BF16)
            mixed = mixed + jnp.dot(ws_ref[0, g].astype(BF16), vg, preferred_element_type=F32)
        o_ref[rows, :] = u * mixed


def _gmlp(proj, mlp_ln_g, mlp_ln_b, w_spatial, b_spatial_t, l):
    return pl.pallas_call(
        _gmlp_kernel,
        grid=(T_ALL // GMLP_TM,),
        in_specs=[
            pl.BlockSpec((GMLP_TM, W_B), lambda i: (i, OFF_UB // W_B)),
            pl.BlockSpec((GMLP_TM, W_B), lambda i: (i, OFF_VB // W_B)),
            pl.BlockSpec((1, 1, W_B), lambda i: (l, 0, 0)),
            pl.BlockSpec((1, 1, W_B), lambda i: (l, 0, 0)),
            pl.BlockSpec((1, G_B, CHUNK_B, CHUNK_B), lambda i: (l, 0, 0, 0)),
            pl.BlockSpec((1, CHUNK_B, G_B), lambda i: (l, 0, 0)),
        ],
        out_specs=pl.BlockSpec((GMLP_TM, W_B), lambda i: (i, 0)),
        out_shape=jax.ShapeDtypeStruct((T_ALL, W_B), F32),
        compiler_params=_cparams("parallel"),
        name="gmlp",
    )(proj, proj, mlp_ln_g.reshape(DEPTH, 1, W_B), mlp_ln_b.reshape(DEPTH, 1, W_B), w_spatial, b_spatial_t)


def _hgrn_kernel(blk_ref, first_ref, s0i_ref, seq_ref, *refs, backward):
    if backward:
        i_ref, f_ref, q_ref, lb_ref, s0_ref, of_ref, gate_ref, g_ref, o_ref, sfin_ref, st_ref = refs
    else:
        i_ref, f_ref, q_ref, lb_ref, s0_ref, o_ref, sfin_ref, st_ref = refs
    step = pl.program_id(0)

    @pl.when(first_ref[step] == 1)
    def _():
        st_ref[...] = s0_ref[0]

    lb = lb_ref[0, 0]
    r = lax.broadcasted_iota(jnp.int32, (HB, HB), 0)
    c = lax.broadcasted_iota(jnp.int32, (HB, HB), 1)
    seen = (c >= r) if backward else (c <= r)
    tri = seen.astype(F32)
    lane = lax.broadcasted_iota(jnp.int32, (HB, WK_C), 1)
    head_masks = [(lane // DK_C) == h for h in range(H_C)]
    rr = lax.broadcasted_iota(jnp.int32, (W_C, WK_C), 0)
    cc = lax.broadcasted_iota(jnp.int32, (W_C, WK_C), 1)
    same_head = (rr // DV_C) == (cc // DK_C)
    order = range(TM // HB - 1, -1, -1) if backward else range(TM // HB)
    for blk in order:
        rows = slice(blk * HB, (blk + 1) * HB)
        f = lb + (1.0 - lb) * jax.nn.sigmoid(f_ref[rows, :])
        logf = jnp.log(f)
        kk = 1.0 - f
        qq = q_ref[rows, :]
        ii = i_ref[rows, :].astype(BF16)
        b = jnp.dot(tri, logf, precision=HIGHEST, preferred_element_type=F32)
        b_end = b[0:1, :] if backward else b[HB - 1:HB, :]
        qe = qq * jnp.exp(b)
        ke = (kk * jnp.exp(-b)).astype(BF16)
        ks = (kk * jnp.exp(b_end - b)).astype(BF16)
        st = st_ref[...]
        o = lax.dot_general(qe.astype(BF16), st.astype(BF16), NT_DIMS, preferred_element_type=F32)
        for h in range(H_C):
            qh = jnp.where(head_masks[h], qe, 0.0).astype(BF16)
            a = lax.dot_general(qh, ke, NT_DIMS, preferred_element_type=F32)
            a = jnp.where(seen, a, 0.0).astype(BF16)
            o = o + jnp.where(head_masks[h], jnp.dot(a, ii, preferred_element_type=F32), 0.0)
        upd = lax.dot_general(ii, ks, TN_DIMS, preferred_element_type=F32)
        st_ref[...] = st * jnp.exp(b_end) + jnp.where(same_head, upd, 0.0)
        if backward:
            o_ref[rows, :] = o + of_ref[rows, :]
        else:
            o_ref[rows, :] = o
    sfin_ref[0] = st_ref[...]
    if backward:
        tot = o_ref[...]
        r2 = lax.broadcasted_iota(jnp.int32, (W_C, W_C), 0)
        c2 = lax.broadcasted_iota(jnp.int32, (W_C, W_C), 1)
        head_mean = jnp.where((r2 // DV_C) == (c2 // DV_C), 1.0 / DV_C, 0.0).astype(F32)
        ms = jnp.dot(tot * tot, head_mean, precision=HIGHEST, preferred_element_type=F32)
        o_ref[...] = tot * lax.rsqrt(ms + LN_EPS) * g_ref[0] * jax.nn.silu(gate_ref[...])


def _hgrn_schedule(backward):
    blk, first, s0i, seq = [], [], [], []
    for s in range(N_SEQ):
        if s < BATCH:
            tiles = [s]
        else:
            base = CTX_TILES + (s - BATCH) * DEC_TILES_PER_SEQ
            tiles = list(range(base, base + DEC_TILES_PER_SEQ))
        if backward:
            tiles = tiles[::-1]
        for n, t in enumerate(tiles):
            blk.append(t)
            first.append(1 if n == 0 else 0)
            s0i.append(0 if s < BATCH else 1 + s - BATCH)
            seq.append(s)
    as_i32 = lambda v: jnp.asarray(v, jnp.int32)
    return as_i32(blk), as_i32(first), as_i32(s0i), as_i32(seq)


def _hgrn_direction(proj, lb_all, s0_dir, l, d, o_fwd=None, hgrn_norm_g=None):
    backward = d == 1
    sched = _hgrn_schedule(backward)
    col = lambda off: off // W_C
    row_spec = lambda cidx: pl.BlockSpec((TM, W_C), lambda i, blk, first, s0i, seq: (blk[i], cidx))
    in_specs = [
        row_spec(col(OFF_IC)),
        row_spec(col(OFF_FC) + d),
        row_spec(col(OFF_QC) + d),
        pl.BlockSpec((1, 1, 1, WK_C), lambda i, blk, first, s0i, seq: (d, l, 0, 0)),
        pl.BlockSpec((1, W_C, WK_C), lambda i, blk, first, s0i, seq: (s0i[i], 0, 0)),
    ]
    args = [proj, proj, proj, lb_all, s0_dir]
    if backward:
        in_specs += [
            row_spec(0),
            row_spec(col(OFF_GC)),
            pl.BlockSpec((1, 1, W_C), lambda i, blk, first, s0i, seq: (l, 0, 0)),
        ]
        args += [o_fwd, proj, hgrn_norm_g.reshape(DEPTH, 1, W_C)]
    grid_spec = pltpu.PrefetchScalarGridSpec(
        num_scalar_prefetch=4,
        grid=(N_TILES,),
        in_specs=in_specs,
        out_specs=[
            row_spec(0),
            pl.BlockSpec((1, W_C, WK_C), lambda i, blk, first, s0i, seq: (seq[i], 0, 0)),
        ],
        scratch_shapes=[pltpu.VMEM((W_C, WK_C), F32)],
    )
    return pl.pallas_call(
        functools.partial(_hgrn_kernel, backward=backward),
        grid_spec=grid_spec,
        out_shape=[
            jax.ShapeDtypeStruct((T_ALL, W_C), F32),
            jax.ShapeDtypeStruct((N_SEQ, W_C, WK_C), F32),
        ],
        compiler_params=_cparams("arbitrary"),
        name="hgrn_bwd" if backward else "hgrn_fwd",
    )(*sched, *args)


def _state_to_blockdiag(s):
    st = jnp.swapaxes(s, -1, -2)
    eye = jnp.eye(H_C, dtype=s.dtype)
    return jnp.einsum('nhed,hg->nhegd', st, eye).reshape(s.shape[0], W_C, WK_C)


def _blockdiag_to_state(st):
    s5 = st.reshape(st.shape[0], H_C, DV_C, H_C, DK_C)
    diag = jnp.stack([s5[:, h, :, h, :] for h in range(H_C)], axis=1)
    return jnp.swapaxes(diag, -1, -2)


def _outproj_kernel(oa_ref, ob_ref, oc_ref, x_ref, mod_ref, w_ref, g_ref, b_ref, wr_ref, x1_ref, h2_ref, lg_ref):
    y = jnp.dot(oa_ref[...].astype(BF16), w_ref[0, 0:W_A, :], preferred_element_type=F32)
    y = y + jnp.dot(ob_ref[...].astype(BF16), w_ref[0, W_A:W_A + W_B, :], preferred_element_type=F32)
    y = y + jnp.dot(oc_ref[...].astype(BF16), w_ref[0, W_A + W_B:MIX_W, :], preferred_element_type=F32)
    gate = mod_ref[0, 0, 2:3, :]
    x1 = _layer_norm_rows(ALPHA * x_ref[...] + gate * y, g_ref[0, 0:1, :], b_ref[0, 0:1, :])
    h2 = x1 * (1.0 + mod_ref[0, 0, 4:5, :]) + mod_ref[0, 0, 3:4, :]
    x1_ref[...] = x1
    h2_ref[...] = h2
    lg_ref[...] = jnp.dot(h2, wr_ref[...], precision=HIGHEST, preferred_element_type=F32)


def _outproj(o_a, o_b, o_c, x, mods, w_out_bf, ln_g, ln_b, w_router, l):
    row = lambda w: pl.BlockSpec((TM, w), lambda i: (i, 0))
    return pl.pallas_call(
        _outproj_kernel,
        grid=(N_TILES,),
        in_specs=[
            row(W_A), row(W_B), row(W_C), row(D_MODEL),
            pl.BlockSpec((1, 1, 6, D_MODEL), lambda i: (l, _tile_group(i), 0, 0)),
            pl.BlockSpec((1, MIX_W, D_MODEL), lambda i: (l, 0, 0)),
            pl.BlockSpec((1, 2, D_MODEL), lambda i: (l, 0, 0)),
            pl.BlockSpec((1, 2, D_MODEL), lambda i: (l, 0, 0)),
            pl.BlockSpec((D_MODEL, N_EXP), lambda i: (0, 0)),
        ],
        out_specs=[row(D_MODEL), row(D_MODEL), row(N_EXP)],
        out_shape=[
            jax.ShapeDtypeStruct((T_ALL, D_MODEL), F32),
            jax.ShapeDtypeStruct((T_ALL, D_MODEL), F32),
            jax.ShapeDtypeStruct((T_ALL, N_EXP), F32),
        ],
        compiler_params=_cparams("parallel"),
        name="outproj",
    )(o_a, o_b, o_c, x, mods, w_out_bf, ln_g, ln_b, w_router)


def _route(logits, router_bias):
    T = logits.shape[0]
    scores = jax.nn.sigmoid(logits)
    sel = scores + router_bias.astype(F32)
    gscore = lax.top_k(sel.reshape(T, N_GROUP, EXP_PER_GROUP), GROUP_TOP)[0].sum(-1)
    gbest = jnp.argmax(gscore, axis=-1)
    in_grp = (jnp.arange(N_EXP) // EXP_PER_GROUP)[None, :] == gbest[:, None]
    _, eidx = lax.top_k(jnp.where(in_grp, sel, -jnp.inf), TOP_K)
    wsel = jnp.take_along_axis(scores, eidx, axis=-1)
    wsel = wsel / wsel.sum(-1, keepdims=True)
    e1, e2 = eidx[:, 0].astype(jnp.int32), eidx[:, 1].astype(jnp.int32)
    swap = e1 > e2
    ea, eb = jnp.where(swap, e2, e1), jnp.where(swap, e1, e2)
    wa = jnp.where(swap, wsel[:, 1], wsel[:, 0])
    wb = jnp.where(swap, wsel[:, 0], wsel[:, 1])
    la, lb = ea % EXP_PER_GROUP, eb % EXP_PER_GROUP
    key = (ea // EXP_PER_GROUP) * (N_PAIR // N_GROUP) + la * (2 * EXP_PER_GROUP - 1 - la) // 2 + (lb - la - 1)

    order = jnp.argsort(key).astype(jnp.int32)
    key_s = key[order]
    counts = jnp.sum(key[:, None] == jnp.arange(N_PAIR)[None, :], axis=0).astype(jnp.int32)
    starts = jnp.cumsum(counts) - counts
    pcounts = (counts + MOE_ROWS - 1) // MOE_ROWS * MOE_ROWS
    pends = jnp.cumsum(pcounts)
    pstarts = pends - pcounts
    dest = pstarts[key_s] + jnp.arange(T, dtype=jnp.int32) - starts[key_s]
    row_tok = jnp.zeros((MOE_NROWS,), jnp.int32).at[dest].set(order)
    row_w = jnp.zeros((MOE_NROWS, 2), F32).at[dest].set(jnp.stack([wa, wb], axis=-1)[order])
    pos = jnp.zeros((T,), jnp.int32).at[order].set(dest)
    block_key = jnp.minimum(
        jnp.searchsorted(pends, jnp.arange(MOE_BLOCKS, dtype=jnp.int32) * MOE_ROWS, side='right'), N_PAIR - 1)
    pk = jnp.arange(N_PAIR)
    loc = pk % (N_PAIR // N_GROUP)
    la_t = jnp.where(loc < 3, 0, jnp.where(loc < 5, 1, 2))
    lb_t = loc - la_t * (2 * EXP_PER_GROUP - 1 - la_t) // 2 + la_t + 1
    ea_t = (pk // (N_PAIR // N_GROUP)) * EXP_PER_GROUP + la_t
    eb_t = (pk // (N_PAIR // N_GROUP)) * EXP_PER_GROUP + lb_t
    block_ea = ea_t[block_key].astype(jnp.int32)
    block_eb = eb_t[block_key].astype(jnp.int32)
    n_used = (pends[-1] // MOE_ROWS).astype(jnp.int32).reshape(1)
    return row_tok, row_w, pos, block_ea, block_eb, n_used


def _start_row_gather(src_hbm, idx_ref, base, buf, sem, n_rows):
    def body(r, carry):
        tok = idx_ref[base + r]
        pltpu.make_async_copy(src_hbm.at[pl.ds(tok, 1)], buf.at[pl.ds(r, 1)], sem).start()
        return carry

    lax.fori_loop(0, n_rows, body, 0)


def _wait_row_gather(src_hbm, buf, sem, n_rows):
    pltpu.make_async_copy(src_hbm.at[pl.ds(0, n_rows)], buf, sem).wait()


def _moe_kernel(tok_ref, ea_ref, eb_ref, nused_ref, h_hbm, rw_ref,
                wga_ref, wua_ref, wda_ref, wgb_ref, wub_ref, wdb_ref, o_ref, xbuf, sems):
    i = pl.program_id(0)
    n_used = nused_ref[0]
    slot = i % 2

    @pl.when(i == 0)
    def _():
        _start_row_gather(h_hbm, tok_ref, 0, xbuf.at[0], sems.at[0], MOE_ROWS)

    @pl.when(i + 1 < n_used)
    def _():
        _start_row_gather(h_hbm, tok_ref, (i + 1) * MOE_ROWS, xbuf.at[1 - slot], sems.at[1 - slot], MOE_ROWS)

    @pl.when(i < n_used)
    def _():
        _wait_row_gather(h_hbm, xbuf.at[slot], sems.at[slot], MOE_ROWS)
        xb = xbuf[slot].astype(BF16)

        def expert(wg_ref, wu_ref, wd_ref):
            g = jnp.dot(xb, wg_ref[0, 0], preferred_element_type=F32)
            u = jnp.dot(xb, wu_ref[0, 0], preferred_element_type=F32)
            act = (jax.nn.silu(g) * u).astype(BF16)
            return jnp.dot(act, wd_ref[0, 0], preferred_element_type=F32)

        rw = rw_ref[...]
        o_ref[...] = rw[:, 0:1] * expert(wga_ref, wua_ref, wda_ref) + rw[:, 1:2] * expert(wgb_ref, wub_ref, wdb_ref)

    @pl.when(i >= n_used)
    def _():
        o_ref[...] = jnp.zeros_like(o_ref)


def _moe(h2, route, wg_bf, wu_bf, wd_bf, l):
    row_tok, row_w, _, block_ea, block_eb, n_used = route
    up = lambda sel: pl.BlockSpec((1, 1, D_MODEL, D_FF_E), lambda i, tok, ea, eb, nu: (l, (ea, eb)[sel][i], 0, 0))
    down = lambda sel: pl.BlockSpec((1, 1, D_FF_E, D_MODEL), lambda i, tok, ea, eb, nu: (l, (ea, eb)[sel][i], 0, 0))
    grid_spec = pltpu.PrefetchScalarGridSpec(
        num_scalar_prefetch=4,
        grid=(MOE_BLOCKS,),
        in_specs=[
            pl.BlockSpec(memory_space=pl.ANY),
            pl.BlockSpec((MOE_ROWS, 2), lambda i, tok, ea, eb, nu: (i, 0)),
            up(0), up(0), down(0), up(1), up(1), down(1),
        ],
        out_specs=pl.BlockSpec((MOE_ROWS, D_MODEL), lambda i, tok, ea, eb, nu: (i, 0)),
        scratch_shapes=[
            pltpu.VMEM((2, MOE_ROWS, D_MODEL), F32),
            pltpu.SemaphoreType.DMA((2,)),
        ],
    )
    return pl.pallas_call(
        _moe_kernel,
        grid_spec=grid_spec,
        out_shape=jax.ShapeDtypeStruct((MOE_NROWS, D_MODEL), F32),
        compiler_params=_cparams("arbitrary"),
        name="moe_experts",
    )(row_tok, block_ea, block_eb, n_used, h2, row_w, wg_bf, wu_bf, wd_bf, wg_bf, wu_bf, wd_bf)


FIN_TM = 128
FIN_TILES = T_ALL // FIN_TM


def _final_kernel(pos_ref, y_hbm, x1_ref, mod_ref, g_ref, b_ref, o_ref, ybuf, sems):
    i = pl.program_id(0)
    slot = i % 2

    @pl.when(i == 0)
    def _():
        _start_row_gather(y_hbm, pos_ref, 0, ybuf.at[0], sems.at[0], FIN_TM)

    @pl.when(i + 1 < FIN_TILES)
    def _():
        _start_row_gather(y_hbm, pos_ref, (i + 1) * FIN_TM, ybuf.at[1 - slot], sems.at[1 - slot], FIN_TM)

    _wait_row_gather(y_hbm, ybuf.at[slot], sems.at[slot], FIN_TM)
    gate = mod_ref[0, 0, 5:6, :]
    o_ref[...] = _layer_norm_rows(ALPHA * x1_ref[...] + gate * ybuf[slot], g_ref[0, 1:2, :], b_ref[0, 1:2, :])


def _final(y_sorted, pos, x1, mods, ln_g, ln_b, l):
    per_group = TM // FIN_TM
    grid_spec = pltpu.PrefetchScalarGridSpec(
        num_scalar_prefetch=1,
        grid=(FIN_TILES,),
        in_specs=[
            pl.BlockSpec(memory_space=pl.ANY),
            pl.BlockSpec((FIN_TM, D_MODEL), lambda i, pos: (i, 0)),
            pl.BlockSpec((1, 1, 6, D_MODEL), lambda i, pos: (l, _tile_group(i // per_group), 0, 0)),
            pl.BlockSpec((1, 2, D_MODEL), lambda i, pos: (l, 0, 0)),
            pl.BlockSpec((1, 2, D_MODEL), lambda i, pos: (l, 0, 0)),
        ],
        out_specs=pl.BlockSpec((FIN_TM, D_MODEL), lambda i, pos: (i, 0)),
        scratch_shapes=[
            pltpu.VMEM((2, FIN_TM, D_MODEL), F32),
            pltpu.SemaphoreType.DMA((2,)),
        ],
    )
    return pl.pallas_call(
        _final_kernel,
        grid_spec=grid_spec,
        out_shape=jax.ShapeDtypeStruct((T_ALL, D_MODEL), F32),
        compiler_params=_cparams("arbitrary"),
        name="moe_combine_ln",
    )(pos, y_sorted, x1, mods, ln_g, ln_b)


def kernel(x_prompt, x_sample, c, cache_k, cache_v, state_hgrn, c_ctx, w_ada, b_ada, w_in, w_out, diff_lambda,
           attn_norm_g, mlp_ln_g, mlp_ln_b, w_spatial, b_spatial, hgrn_lb, hgrn_norm_g, ln_g, ln_b, w_router,
           router_bias, w_gate, w_up, w_down):
    x = jnp.concatenate([x_prompt.reshape(T_CTX, D_MODEL), x_sample.reshape(T_DEC, D_MODEL)], axis=0)
    cond = jnp.concatenate([c_ctx[None, :], c, jnp.zeros((COND_PAD - N_COND, D_MODEL), F32)], axis=0)
    mods = _ada_modulation(cond, w_ada, b_ada)

    w_in_bf, w_out_bf = w_in.astype(BF16), w_out.astype(BF16)
    wg_bf, wu_bf, wd_bf = w_gate.astype(BF16), w_up.astype(BF16), w_down.astype(BF16)
    cos_t, sin_t = _rope_tables()
    cache_k4 = cache_k.reshape(DEC_BATCH, DEPTH, PAST_LEN, QK_W_A)
    cache_v4 = cache_v.reshape(DEC_BATCH, DEPTH, PAST_LEN, W_A)
    b_spatial_t = jnp.swapaxes(b_spatial, 1, 2)

    p = jax.nn.softmax(hgrn_lb.astype(F32), axis=1)
    cs = jnp.cumsum(p, axis=1)
    lb_all = (cs - cs[:, :1]).reshape(2, DEPTH, 1, WK_C)

    new_k, new_v, new_s = [], [], []
    for l in range(DEPTH):
        proj = _inproj(x, mods, w_in_bf, cos_t, sin_t, l)
        o_a = jnp.concatenate([
            _attention(proj, None, None, diff_lambda, attn_norm_g, l, decode=False),
            _attention(proj, cache_k4, cache_v4, diff_lambda, attn_norm_g, l, decode=True)], axis=0)
        o_b = _gmlp(proj, mlp_ln_g, mlp_ln_b, w_spatial, b_spatial_t, l)
        s0 = [jnp.concatenate([jnp.zeros((1, W_C, WK_C), F32), _state_to_blockdiag(state_hgrn[:, l, d])], axis=0)
              for d in range(2)]
        o_f, s_f = _hgrn_direction(proj, lb_all, s0[0], l, 0)
        o_c, s_b = _hgrn_direction(proj, lb_all, s0[1], l, 1, o_fwd=o_f, hgrn_norm_g=hgrn_norm_g)
        x1, h2, logits = _outproj(o_a, o_b, o_c, x, mods, w_out_bf, ln_g, ln_b, w_router, l)
        route = _route(logits, router_bias)
        y_sorted = _moe(h2, route, wg_bf, wu_bf, wd_bf, l)
        x = _final(y_sorted, route[2], x1, mods, ln_g, ln_b, l)

        new_k.append(proj[:T_CTX, OFF_KA:OFF_KA + QK_W_A].reshape(BATCH, SEQ, H_A, 2, DQK_A))
        new_v.append(proj[:T_CTX, OFF_VA:OFF_VA + W_A].reshape(BATCH, SEQ, H_A, DV_A))
        new_s.append(jnp.stack([_blockdiag_to_state(s_f[:BATCH]), _blockdiag_to_state(s_b[:BATCH])], axis=1))

    return (x[:T_CTX].reshape(BATCH, SEQ, D_MODEL), x[T_CTX:].reshape(DEC_BATCH, DEC_SEQ, D_MODEL),
            jnp.stack(new_k, axis=1), jnp.stack(new_v, axis=1), jnp.stack(new_s, axis=1))
```

```python
import functools
import math

import jax
import jax.numpy as jnp
from jax import lax
from jax.experimental import pallas as pl
from jax.experimental.pallas import tpu as pltpu

D_MODEL = 1024
BATCH = 16
SEQ = 256
DEPTH = 4
DEC_BATCH = 4
DEC_SEQ = 4096
PAST_LEN = 512
GRID_W = 64
H_A = 4
DQK_A = 64
DV_A = 128
QK_W_A = H_A * 2 * DQK_A
W_A = H_A * DV_A
G_B = 4
DG_B = 64
W_B = G_B * DG_B
CHUNK_B = 128
H_C = 4
DK_C = 64
DV_C = 64
WK_C = H_C * DK_C
W_C = H_C * DV_C
MIX_W = W_A + W_B + W_C
OFF_QA = 0
OFF_KA = OFF_QA + QK_W_A
OFF_VA = OFF_KA + QK_W_A
OFF_UB = OFF_VA + W_A
OFF_VB = OFF_UB + W_B
OFF_IC = OFF_VB + W_B
OFF_FC = OFF_IC + W_C
OFF_QC = OFF_FC + 2 * WK_C
OFF_GC = OFF_QC + 2 * WK_C
N_IN = OFF_GC + W_C
ROPE_THETA = 10000.0
N_EXP = 16
N_GROUP = 4
EXP_PER_GROUP = N_EXP // N_GROUP
TOP_K = 2
GROUP_TOP = 2
D_FF_E = 512
LN_EPS = 1e-5
ALPHA = (2 * DEPTH) ** 0.25

T_CTX = BATCH * SEQ
T_DEC = DEC_BATCH * DEC_SEQ
T_ALL = T_CTX + T_DEC
N_SEQ = BATCH + DEC_BATCH
N_COND = 1 + DEC_BATCH
COND_PAD = 8

LANES = 128
TM = 256
CTX_TILES = T_CTX // TM
DEC_TILES_PER_SEQ = DEC_SEQ // TM
N_TILES = T_ALL // TM
HB = 16
MOE_ROWS = 128
N_PAIR = N_GROUP * (EXP_PER_GROUP * (EXP_PER_GROUP - 1) // 2)
MOE_BLOCKS = T_ALL // MOE_ROWS + N_PAIR
MOE_NROWS = MOE_BLOCKS * MOE_ROWS
VMEM_LIMIT = 56 * 1024 * 1024

BF16 = jnp.bfloat16
F32 = jnp.float32
HIGHEST = lax.Precision.HIGHEST
NT_DIMS = (((1,), (1,)), ((), ()))
TN_DIMS = (((0,), (0,)), ((), ()))


def _cparams(*sem):
    return pltpu.CompilerParams(dimension_semantics=sem, vmem_limit_bytes=VMEM_LIMIT)


def _tile_group(i):
    return jnp.where(i < CTX_TILES, 0, 1 + (i - CTX_TILES) // DEC_TILES_PER_SEQ)


def _layer_norm_rows(z, g, b):
    mu = jnp.mean(z, axis=-1, keepdims=True)
    zc = z - mu
    var = jnp.mean(zc * zc, axis=-1, keepdims=True)
    return zc * lax.rsqrt(var + LN_EPS) * g + b


ADA_TN = 1536


def _ada_kernel(cond_ref, w_ref, b_ref, o_ref):
    a = jax.nn.silu(cond_ref[...]).astype(BF16)
    o_ref[0] = jnp.dot(a, w_ref[0].astype(BF16), preferred_element_type=F32) + b_ref[0]


def _ada_modulation(cond, w_ada, b_ada):
    n = 6 * D_MODEL
    out = pl.pallas_call(
        _ada_kernel,
        grid=(DEPTH, n // ADA_TN),
        in_specs=[
            pl.BlockSpec((COND_PAD, D_MODEL), lambda l, j: (0, 0)),
            pl.BlockSpec((1, D_MODEL, ADA_TN), lambda l, j: (l, 0, j)),
            pl.BlockSpec((1, 1, ADA_TN), lambda l, j: (l, 0, j)),
        ],
        out_specs=pl.BlockSpec((1, COND_PAD, ADA_TN), lambda l, j: (l, 0, j)),
        out_shape=jax.ShapeDtypeStruct((DEPTH, COND_PAD, n), F32),
        compiler_params=_cparams("parallel", "parallel"),
        name="ada_modulation",
    )(cond, w_ada, b_ada.reshape(DEPTH, 1, n))
    return out.reshape(DEPTH, COND_PAD, 6, D_MODEL)


INPROJ_TN = 512


def _rope_tables():
    pos = jnp.arange(DEC_SEQ)
    row = (pos // GRID_W).astype(F32)
    col = (pos % GRID_W).astype(F32)
    quarter = DQK_A // 4
    inv = ROPE_THETA ** (-jnp.arange(quarter, dtype=F32) / quarter)
    lane = jnp.arange(LANES)
    in_head = lane % DQK_A
    use_col = in_head >= DQK_A // 2
    freq = inv[lane % quarter]
    ang = jnp.where(use_col[None, :], col[:, None], row[:, None]) * freq[None, :]
    sign = jnp.where((lane % (2 * quarter)) < quarter, -1.0, 1.0).astype(F32)
    cos = jnp.concatenate([jnp.cos(ang), jnp.ones((TM, LANES), F32)], axis=0)
    sin = jnp.concatenate([jnp.sin(ang) * sign[None, :], jnp.zeros((TM, LANES), F32)], axis=0)
    return cos, sin


def _inproj_kernel(x_ref, mod_ref, w_ref, cos_ref, sin_ref, o_ref):
    shift = mod_ref[0, 0, 0:1, :]
    scale = mod_ref[0, 0, 1:2, :]
    h = (x_ref[...] * (1.0 + scale) + shift).astype(BF16)
    cos = cos_ref[...]
    sin = sin_ref[...]
    lane = lax.broadcasted_iota(jnp.int32, (TM, LANES), 1)
    first = (lane % (DQK_A // 2)) < (DQK_A // 4)
    for j in range(N_IN // INPROJ_TN):
        acc = jnp.dot(h, w_ref[0, :, j * INPROJ_TN:(j + 1) * INPROJ_TN], preferred_element_type=F32)
        if (j + 1) * INPROJ_TN <= OFF_VA:
            for cb in range(INPROJ_TN // LANES):
                blk = acc[:, cb * LANES:(cb + 1) * LANES]
                partner = jnp.where(first, pltpu.roll(blk, LANES - DQK_A // 4, 1), pltpu.roll(blk, DQK_A // 4, 1))
                o_ref[:, j * INPROJ_TN + cb * LANES:j * INPROJ_TN + (cb + 1) * LANES] = blk * cos + partner * sin
        else:
            o_ref[:, j * INPROJ_TN:(j + 1) * INPROJ_TN] = acc


def _inproj(x, mods, w_in_bf, cos_t, sin_t, l):
    def rope_blk(i):
        return jnp.where(i < CTX_TILES, DEC_TILES_PER_SEQ, (i - CTX_TILES) % DEC_TILES_PER_SEQ)

    return pl.pallas_call(
        _inproj_kernel,
        grid=(N_TILES,),
        in_specs=[
            pl.BlockSpec((TM, D_MODEL), lambda i: (i, 0)),
            pl.BlockSpec((1, 1, 6, D_MODEL), lambda i: (l, _tile_group(i), 0, 0)),
            pl.BlockSpec((1, D_MODEL, N_IN), lambda i: (l, 0, 0)),
            pl.BlockSpec((TM, LANES), lambda i: (rope_blk(i), 0)),
            pl.BlockSpec((TM, LANES), lambda i: (rope_blk(i), 0)),
        ],
        out_specs=pl.BlockSpec((TM, N_IN), lambda i: (i, 0)),
        out_shape=jax.ShapeDtypeStruct((T_ALL, N_IN), F32),
        compiler_params=_cparams("arbitrary"),
        name="inproj",
    )(x, mods, w_in_bf, cos_t, sin_t)


def _attn_kernel(*refs, lam_init, has_cache):
    if has_cache:
        q_ref, kn_ref, vn_ref, kc_ref, vc_ref, dl_ref, g_ref, o_ref = refs
    else:
        q_ref, kn_ref, vn_ref, dl_ref, g_ref, o_ref = refs
    dl = dl_ref[0]
    lam = (jnp.exp(jnp.sum(dl[0:1] * dl[1:2], axis=-1, keepdims=True))
           - jnp.exp(jnp.sum(dl[2:3] * dl[3:4], axis=-1, keepdims=True)) + lam_init)
    q = q_ref[...] * (DQK_A ** -0.5 * math.log2(math.e))
    lane = lax.broadcasted_iota(jnp.int32, q.shape, 1)
    keys = [kn_ref[...].astype(BF16)]
    vals = [vn_ref[...].astype(BF16)]
    if has_cache:
        keys.append(kc_ref[0, 0].astype(BF16))
        vals.append(vc_ref[0, 0].astype(BF16))
    outs = []
    for m in range(2):
        qm = jnp.where((lane // DQK_A) == m, q, 0.0).astype(BF16)
        ss = [lax.dot_general(qm, k, NT_DIMS, preferred_element_type=F32) for k in keys]
        mx = functools.reduce(jnp.maximum, [jnp.max(s, axis=-1, keepdims=True) for s in ss])
        ps = [jnp.exp2(s - mx) for s in ss]
        den = functools.reduce(jnp.add, [jnp.sum(p, axis=-1, keepdims=True) for p in ps])
        acc = functools.reduce(jnp.add, [jnp.dot(p.astype(BF16), v, preferred_element_type=F32)
                                         for p, v in zip(ps, vals)])
        outs.append(acc / den)
    o = outs[0] - lam * outs[1]
    ms = jnp.mean(o * o, axis=-1, keepdims=True)
    o_ref[...] = o * lax.rsqrt(ms + LN_EPS) * g_ref[0] * (1.0 - lam_init)


def _attention(proj, cache_k4, cache_v4, diff_lambda, attn_norm_g, l, decode):
    lam_init = 0.8 - 0.6 * math.exp(-0.3 * l)
    qcol, kcol, vcol = OFF_QA // LANES, OFF_KA // LANES, OFF_VA // LANES
    common = [
        pl.BlockSpec((1, 4, DQK_A), lambda b, h, i: (l, 0, 0)),
        pl.BlockSpec((1, 1, DV_A), lambda b, h, i: (l, 0, 0)),
    ]
    if decode:
        seq_blk0 = T_CTX // DEC_SEQ
        grid = (DEC_BATCH, H_A, DEC_TILES_PER_SEQ)
        in_specs = [
            pl.BlockSpec((TM, LANES), lambda b, h, i: (CTX_TILES + b * DEC_TILES_PER_SEQ + i, qcol + h)),
            pl.BlockSpec((DEC_SEQ, LANES), lambda b, h, i: (seq_blk0 + b, kcol + h)),
            pl.BlockSpec((DEC_SEQ, LANES), lambda b, h, i: (seq_blk0 + b, vcol + h)),
            pl.BlockSpec((1, 1, PAST_LEN, LANES), lambda b, h, i: (b, l, 0, h)),
            pl.BlockSpec((1, 1, PAST_LEN, LANES), lambda b, h, i: (b, l, 0, h)),
        ] + common
        args = (proj, proj, proj, cache_k4, cache_v4)
        out_rows = T_DEC
        out_spec = pl.BlockSpec((TM, LANES), lambda b, h, i: (b * DEC_TILES_PER_SEQ + i, h))
    else:
        grid = (BATCH, H_A, 1)
        in_specs = [
            pl.BlockSpec((SEQ, LANES), lambda b, h, i: (b, qcol + h)),
            pl.BlockSpec((SEQ, LANES), lambda b, h, i: (b, kcol + h)),
            pl.BlockSpec((SEQ, LANES), lambda b, h, i: (b, vcol + h)),
        ] + common
        args = (proj, proj, proj)
        out_rows = T_CTX
        out_spec = pl.BlockSpec((SEQ, LANES), lambda b, h, i: (b, h))
    return pl.pallas_call(
        functools.partial(_attn_kernel, lam_init=lam_init, has_cache=decode),
        grid=grid,
        in_specs=in_specs,
        out_specs=out_spec,
        out_shape=jax.ShapeDtypeStruct((out_rows, W_A), F32),
        compiler_params=_cparams("parallel", "parallel", "arbitrary"),
        name="attn_dec" if decode else "attn_ctx",
    )(*args, diff_lambda, attn_norm_g.reshape(DEPTH, 1, DV_A))


GMLP_TM = 512


def _gmlp_kernel(u_ref, v_ref, g_ref, b_ref, ws_ref, bs_ref, o_ref):
    lane = lax.broadcasted_iota(jnp.int32, (CHUNK_B, W_B), 1)
    masks = [(lane // DG_B) == g for g in range(G_B)]
    bias = functools.reduce(
        jnp.add, [jnp.where(masks[g], bs_ref[0, :, g:g + 1], 0.0) for g in range(G_B)])
    for c in range(GMLP_TM // CHUNK_B):
        rows = slice(c * CHUNK_B, (c + 1) * CHUNK_B)
        u = jax.nn.gelu(u_ref[rows, :])
        v = _layer_norm_rows(jax.nn.gelu(v_ref[rows, :]), g_ref[0], b_ref[0])
        mixed = bias
        for g in range(G_B):
            vg = jnp.where(masks[g], v, 0.0).astype(BF16)
            mixed = mixed + jnp.dot(ws_ref[0, g].astype(BF16), vg, preferred_element_type=F32)
        o_ref[rows, :] = u * mixed


def _gmlp(proj, mlp_ln_g, mlp_ln_b, w_spatial, b_spatial_t, l):
    return pl.pallas_call(
        _gmlp_kernel,
        grid=(T_ALL // GMLP_TM,),
        in_specs=[
            pl.BlockSpec((GMLP_TM, W_B), lambda i: (i, OFF_UB // W_B)),
            pl.BlockSpec((GMLP_TM, W_B), lambda i: (i, OFF_VB // W_B)),
            pl.BlockSpec((1, 1, W_B), lambda i: (l, 0, 0)),
            pl.BlockSpec((1, 1, W_B), lambda i: (l, 0, 0)),
            pl.BlockSpec((1, G_B, CHUNK_B, CHUNK_B), lambda i: (l, 0, 0, 0)),
            pl.BlockSpec((1, CHUNK_B, G_B), lambda i: (l, 0, 0)),
        ],
        out_specs=pl.BlockSpec((GMLP_TM, W_B), lambda i: (i, 0)),
        out_shape=jax.ShapeDtypeStruct((T_ALL, W_B), F32),
        compiler_params=_cparams("parallel"),
        name="gmlp",
    )(proj, proj, mlp_ln_g.reshape(DEPTH, 1, W_B), mlp_ln_b.reshape(DEPTH, 1, W_B), w_spatial, b_spatial_t)


def _hgrn_kernel(blk_ref, first_ref, s0i_ref, seq_ref, *refs, backward):
    if backward:
        i_ref, f_ref, q_ref, lb_ref, s0_ref, of_ref, gate_ref, g_ref, o_ref, sfin_ref, st_ref = refs
    else:
        i_ref, f_ref, q_ref, lb_ref, s0_ref, o_ref, sfin_ref, st_ref = refs
    step = pl.program_id(0)

    @pl.when(first_ref[step] == 1)
    def _():
        st_ref[...] = s0_ref[0]

    lb = lb_ref[0, 0]
    r = lax.broadcasted_iota(jnp.int32, (HB, HB), 0)
    c = lax.broadcasted_iota(jnp.int32, (HB, HB), 1)
    seen = (c >= r) if backward else (c <= r)
    tri = seen.astype(F32)
    lane = lax.broadcasted_iota(jnp.int32, (HB, WK_C), 1)
    head_masks = [(lane // DK_C) == h for h in range(H_C)]
    rr = lax.broadcasted_iota(jnp.int32, (W_C, WK_C), 0)
    cc = lax.broadcasted_iota(jnp.int32, (W_C, WK_C), 1)
    same_head = (rr // DV_C) == (cc // DK_C)
    order = range(TM // HB - 1, -1, -1) if backward else range(TM // HB)
    for blk in order:
        rows = slice(blk * HB, (blk + 1) * HB)
        f = lb + (1.0 - lb) * jax.nn.sigmoid(f_ref[rows, :])
        logf = jnp.log(f)
        kk = 1.0 - f
        qq = q_ref[rows, :]
        ii = i_ref[rows, :].astype(BF16)
        b = jnp.dot(tri, logf, precision=HIGHEST, preferred_element_type=F32)
        b_end = b[0:1, :] if backward else b[HB - 1:HB, :]
        qe = qq * jnp.exp(b)
        ke = (kk * jnp.exp(-b)).astype(BF16)
        ks = (kk * jnp.exp(b_end - b)).astype(BF16)
        st = st_ref[...]
        o = lax.dot_general(qe.astype(BF16), st.astype(BF16), NT_DIMS, preferred_element_type=F32)
        for h in range(H_C):
            qh = jnp.where(head_masks[h], qe, 0.0).astype(BF16)
            a = lax.dot_general(qh, ke, NT_DIMS, preferred_element_type=F32)
            a = jnp.where(seen, a, 0.0).astype(BF16)
            o = o + jnp.where(head_masks[h], jnp.dot(a, ii, preferred_element_type=F32), 0.0)
        upd = lax.dot_general(ii, ks, TN_DIMS, preferred_element_type=F32)
        st_ref[...] = st * jnp.exp(b_end) + jnp.where(same_head, upd, 0.0)
        if backward:
            o_ref[rows, :] = o + of_ref[rows, :]
        else:
            o_ref[rows, :] = o
    sfin_ref[0] = st_ref[...]
    if backward:
        tot = o_ref[...]
        r2 = lax.broadcasted_iota(jnp.int32, (W_C, W_C), 0)
        c2 = lax.broadcasted_iota(jnp.int32, (W_C, W_C), 1)
        head_mean = jnp.where((r2 // DV_C) == (c2 // DV_C), 1.0 / DV_C, 0.0).astype(F32)
        ms = jnp.dot(tot * tot, head_mean, precision=HIGHEST, preferred_element_type=F32)
        o_ref[...] = tot * lax.rsqrt(ms + LN_EPS) * g_ref[0] * jax.nn.silu(gate_ref[...])


def _hgrn_schedule(backward):
    blk, first, s0i, seq = [], [], [], []
    for s in range(N_SEQ):
        if s < BATCH:
            tiles = [s]
        else:
            base = CTX_TILES + (s - BATCH) * DEC_TILES_PER_SEQ
            tiles = list(range(base, base + DEC_TILES_PER_SEQ))
        if backward:
            tiles = tiles[::-1]
        for n, t in enumerate(tiles):
            blk.append(t)
            first.append(1 if n == 0 else 0)
            s0i.append(0 if s < BATCH else 1 + s - BATCH)
            seq.append(s)
    as_i32 = lambda v: jnp.asarray(v, jnp.int32)
    return as_i32(blk), as_i32(first), as_i32(s0i), as_i32(seq)


def _hgrn_direction(proj, lb_all, s0_dir, l, d, o_fwd=None, hgrn_norm_g=None):
    backward = d == 1
    sched = _hgrn_schedule(backward)
    col = lambda off: off // W_C
    row_spec = lambda cidx: pl.BlockSpec((TM, W_C), lambda i, blk, first, s0i, seq: (blk[i], cidx))
    in_specs = [
        row_spec(col(OFF_IC)),
        row_spec(col(OFF_FC) + d),
        row_spec(col(OFF_QC) + d),
        pl.BlockSpec((1, 1, 1, WK_C), lambda i, blk, first, s0i, seq: (d, l, 0, 0)),
        pl.BlockSpec((1, W_C, WK_C), lambda i, blk, first, s0i, seq: (s0i[i], 0, 0)),
    ]
    args = [proj, proj, proj, lb_all, s0_dir]
    if backward:
        in_specs += [
            row_spec(0),
            row_spec(col(OFF_GC)),
            pl.BlockSpec((1, 1, W_C), lambda i, blk, first, s0i, seq: (l, 0, 0)),
        ]
        args += [o_fwd, proj, hgrn_norm_g.reshape(DEPTH, 1, W_C)]
    grid_spec = pltpu.PrefetchScalarGridSpec(
        num_scalar_prefetch=4,
        grid=(N_TILES,),
        in_specs=in_specs,
        out_specs=[
            row_spec(0),
            pl.BlockSpec((1, W_C, WK_C), lambda i, blk, first, s0i, seq: (seq[i], 0, 0)),
        ],
        scratch_shapes=[pltpu.VMEM((W_C, WK_C), F32)],
    )
    return pl.pallas_call(
        functools.partial(_hgrn_kernel, backward=backward),
        grid_spec=grid_spec,
        out_shape=[
            jax.ShapeDtypeStruct((T_ALL, W_C), F32),
            jax.ShapeDtypeStruct((N_SEQ, W_C, WK_C), F32),
        ],
        compiler_params=_cparams("arbitrary"),
        name="hgrn_bwd" if backward else "hgrn_fwd",
    )(*sched, *args)


def _state_to_blockdiag(s):
    st = jnp.swapaxes(s, -1, -2)
    eye = jnp.eye(H_C, dtype=s.dtype)
    return jnp.einsum('nhed,hg->nhegd', st, eye).reshape(s.shape[0], W_C, WK_C)


def _blockdiag_to_state(st):
    s5 = st.reshape(st.shape[0], H_C, DV_C, H_C, DK_C)
    diag = jnp.stack([s5[:, h, :, h, :] for h in range(H_C)], axis=1)
    return jnp.swapaxes(diag, -1, -2)


def _outproj_kernel(oa_ref, ob_ref, oc_ref, x_ref, mod_ref, w_ref, g_ref, b_ref, wr_ref, x1_ref, h2_ref, lg_ref):
    y = jnp.dot(oa_ref[...].astype(BF16), w_ref[0, 0:W_A, :], preferred_element_type=F32)
    y = y + jnp.dot(ob_ref[...].astype(BF16), w_ref[0, W_A:W_A + W_B, :], preferred_element_type=F32)
    y = y + jnp.dot(oc_ref[...].astype(BF16), w_ref[0, W_A + W_B:MIX_W, :], preferred_element_type=F32)
    gate = mod_ref[0, 0, 2:3, :]
    x1 = _layer_norm_rows(ALPHA * x_ref[...] + gate * y, g_ref[0, 0:1, :], b_ref[0, 0:1, :])
    h2 = x1 * (1.0 + mod_ref[0, 0, 4:5, :]) + mod_ref[0, 0, 3:4, :]
    x1_ref[...] = x1
    h2_ref[...] = h2
    lg_ref[0] = lax.dot_general(wr_ref[...], h2, NT_DIMS, precision=HIGHEST, preferred_element_type=F32)


def _outproj(o_a, o_b, o_c, x, mods, w_out_bf, ln_g, ln_b, w_router_t, l):
    row = lambda w: pl.BlockSpec((TM, w), lambda i: (i, 0))
    return pl.pallas_call(
        _outproj_kernel,
        grid=(N_TILES,),
        in_specs=[
            row(W_A), row(W_B), row(W_C), row(D_MODEL),
            pl.BlockSpec((1, 1, 6, D_MODEL), lambda i: (l, _tile_group(i), 0, 0)),
            pl.BlockSpec((1, MIX_W, D_MODEL), lambda i: (l, 0, 0)),
            pl.BlockSpec((1, 2, D_MODEL), lambda i: (l, 0, 0)),
            pl.BlockSpec((1, 2, D_MODEL), lambda i: (l, 0, 0)),
            pl.BlockSpec((N_EXP, D_MODEL), lambda i: (0, 0)),
        ],
        out_specs=[row(D_MODEL), row(D_MODEL), pl.BlockSpec((1, N_EXP, TM), lambda i: (i, 0, 0))],
        out_shape=[
            jax.ShapeDtypeStruct((T_ALL, D_MODEL), F32),
            jax.ShapeDtypeStruct((T_ALL, D_MODEL), F32),
            jax.ShapeDtypeStruct((N_TILES, N_EXP, TM), F32),
        ],
        compiler_params=_cparams("parallel"),
        name="outproj",
    )(o_a, o_b, o_c, x, mods, w_out_bf, ln_g, ln_b, w_router_t)


KEY_PAD = 32
PAIRS_PER_GROUP = N_PAIR // N_GROUP
assert EXP_PER_GROUP == 4 and TOP_K == 2 and GROUP_TOP == 2 and MOE_BLOCKS <= TM


def _pair_index(la, lb):
    return jnp.right_shift(la * (2 * EXP_PER_GROUP - 1 - la), 1) + (lb - la - 1)


def _route_kernel(lg_ref, bias_ref, dest_ref, wa_ref, wb_ref, bea_ref, beb_ref, nused_ref, key_scr, rank_scr):
    bias = bias_ref[...]
    kio = lax.broadcasted_iota(jnp.int32, (KEY_PAD, TM), 0)
    tr = lax.broadcasted_iota(jnp.int32, (TM, TM), 0)
    tc = lax.broadcasted_iota(jnp.int32, (TM, TM), 1)
    upto = jnp.where(tr <= tc, 1.0, 0.0).astype(BF16)

    def select(ti, counts):
        scores = jax.nn.sigmoid(lg_ref[ti])
        sel = scores + bias
        row = lambda a, e: a[e:e + 1, :]
        gscore = []
        for g in range(N_GROUP):
            v = [row(sel, EXP_PER_GROUP * g + j) for j in range(EXP_PER_GROUP)]
            sums = [v[a] + v[b] for a in range(EXP_PER_GROUP) for b in range(a + 1, EXP_PER_GROUP)]
            gscore.append(functools.reduce(jnp.maximum, sums))
        gbest = jnp.zeros((1, TM), jnp.int32)
        best = gscore[0]
        for g in range(1, N_GROUP):
            better = gscore[g] > best
            gbest = jnp.where(better, g, gbest)
            best = jnp.where(better, gscore[g], best)
        vb, sb = [], []
        for j in range(EXP_PER_GROUP):
            vj, sj = row(sel, j), row(scores, j)
            for g in range(1, N_GROUP):
                vj = jnp.where(gbest == g, row(sel, EXP_PER_GROUP * g + j), vj)
                sj = jnp.where(gbest == g, row(scores, EXP_PER_GROUP * g + j), sj)
            vb.append(vj)
            sb.append(sj)
        one, zero = jnp.ones((1, TM), jnp.int32), jnp.zeros((1, TM), jnp.int32)
        rank = []
        for j in range(EXP_PER_GROUP):
            rj = zero
            for i in range(EXP_PER_GROUP):
                if i != j:
                    ahead = (vb[i] >= vb[j]) if i < j else (vb[i] > vb[j])
                    rj = rj + jnp.where(ahead, one, zero)
            rank.append(rj)
        pick_i = lambda r: functools.reduce(jnp.add, [jnp.where(rank[j] == r, j, 0) for j in range(EXP_PER_GROUP)])
        pick_s = lambda r: functools.reduce(jnp.add, [jnp.where(rank[j] == r, sb[j], 0.0) for j in range(EXP_PER_GROUP)])
        t1, t2, s1, s2 = pick_i(0), pick_i(1), pick_s(0), pick_s(1)
        den = s1 + s2
        swap = t1 > t2
        la, lb = jnp.minimum(t1, t2), jnp.maximum(t1, t2)
        wa_ref[ti] = jnp.where(swap, s2, s1) / den
        wb_ref[ti] = jnp.where(swap, s1, s2) / den
        key = gbest * PAIRS_PER_GROUP + _pair_index(la, lb)
        onehot = kio == key
        ahead_incl = jnp.dot(jnp.where(onehot, 1.0, 0.0).astype(BF16), upto, preferred_element_type=F32)
        key_scr[ti] = key
        rank_scr[ti] = jnp.sum(jnp.where(onehot, ahead_incl - 1.0 + counts, 0.0), axis=0, keepdims=True)
        return counts + ahead_incl[:, TM - 1:TM]

    counts = lax.fori_loop(0, N_TILES, select, jnp.zeros((KEY_PAD, TM), F32))
    pcounts = jnp.floor((counts + (MOE_ROWS - 1)) * (1.0 / MOE_ROWS)) * MOE_ROWS
    kr = lax.broadcasted_iota(jnp.int32, (KEY_PAD, KEY_PAD), 0)
    kc = lax.broadcasted_iota(jnp.int32, (KEY_PAD, KEY_PAD), 1)
    pends = jnp.dot(jnp.where(kc <= kr, 1.0, 0.0), pcounts, precision=HIGHEST, preferred_element_type=F32)
    pstarts = pends - pcounts

    def place(ti, carry):
        onehot = kio == key_scr[ti]
        dest = jnp.sum(jnp.where(onehot, pstarts, 0.0), axis=0, keepdims=True) + rank_scr[ti]
        dest_ref[ti] = dest.astype(jnp.int32)
        return carry

    lax.fori_loop(0, N_TILES, place, 0)

    start = lax.broadcasted_iota(jnp.int32, (KEY_PAD, TM), 1).astype(F32) * MOE_ROWS
    ended = jnp.where(kio < N_PAIR, jnp.where(pends <= start, 1, 0), 0)
    bkey = jnp.minimum(jnp.sum(ended, axis=0, keepdims=True), N_PAIR - 1)
    grp = functools.reduce(jnp.add, [jnp.where(bkey >= PAIRS_PER_GROUP * g, 1, 0) for g in range(1, N_GROUP)])
    loc = bkey - PAIRS_PER_GROUP * grp
    la = jnp.where(loc >= 3, 1, 0) + jnp.where(loc >= 5, 1, 0)
    lb = loc - _pair_index(la, la + 1) + la + 1
    bea_ref[...] = EXP_PER_GROUP * grp + la
    beb_ref[...] = EXP_PER_GROUP * grp + lb
    nused_ref[...] = (pends[N_PAIR - 1:N_PAIR, :] * (1.0 / MOE_ROWS)).astype(jnp.int32)


def _route(logits_t, router_bias):
    whole = lambda shape: pl.BlockSpec(shape, lambda i: (0,) * len(shape))
    tok = (N_TILES, 1, TM)
    dest, wa, wb, bea, beb, nused = pl.pallas_call(
        _route_kernel,
        grid=(1,),
        in_specs=[whole((N_TILES, N_EXP, TM)), whole((N_EXP, 1))],
        out_specs=[whole(tok), whole(tok), whole(tok), whole((1, TM)), whole((1, TM)), whole((1, TM))],
        out_shape=[
            jax.ShapeDtypeStruct(tok, jnp.int32), jax.ShapeDtypeStruct(tok, F32), jax.ShapeDtypeStruct(tok, F32),
            jax.ShapeDtypeStruct((1, TM), jnp.int32), jax.ShapeDtypeStruct((1, TM), jnp.int32),
            jax.ShapeDtypeStruct((1, TM), jnp.int32),
        ],
        scratch_shapes=[pltpu.VMEM(tok, jnp.int32), pltpu.VMEM(tok, F32)],
        compiler_params=_cparams("arbitrary"),
        name="route",
    )(logits_t, router_bias.astype(F32).reshape(N_EXP, 1))
    wab = jnp.stack([wa.reshape(T_ALL), wb.reshape(T_ALL)], axis=-1)
    return dest.reshape(T_ALL), wab, bea[0, :MOE_BLOCKS], beb[0, :MOE_BLOCKS], nused[0, :1]


XS_W = D_MODEL + LANES


def _permute_kernel(dest_ref, h_ref, wab_ref, xs_in_hbm, xs_hbm, stage, sems):
    del xs_in_hbm
    i = pl.program_id(0)
    slot = i % 2

    def wait_slot(s):
        pltpu.make_async_copy(stage.at[s], xs_hbm.at[pl.ds(0, TM)], sems.at[s]).wait()

    @pl.when(i >= 2)
    def _():
        wait_slot(slot)

    stage[slot, :, 0:D_MODEL] = h_ref[...]
    lane = lax.broadcasted_iota(jnp.int32, (TM, LANES), 1)
    w = wab_ref[...]
    stage[slot, :, D_MODEL:XS_W] = jnp.where(lane == 0, w[:, 0:1], jnp.where(lane == 1, w[:, 1:2], 0.0))

    def body(r, carry):
        d = dest_ref[i * TM + r]
        pltpu.make_async_copy(stage.at[slot, pl.ds(r, 1)], xs_hbm.at[pl.ds(d, 1)], sems.at[slot]).start()
        return carry

    lax.fori_loop(0, TM, body, 0)

    @pl.when(i == N_TILES - 1)
    def _():
        wait_slot(1 - slot)
        wait_slot(slot)


def _permute(h2, dest, wab):
    grid_spec = pltpu.PrefetchScalarGridSpec(
        num_scalar_prefetch=1,
        grid=(N_TILES,),
        in_specs=[
            pl.BlockSpec((TM, D_MODEL), lambda i, dest: (i, 0)),
            pl.BlockSpec((TM, 2), lambda i, dest: (i, 0)),
            pl.BlockSpec(memory_space=pl.ANY),
        ],
        out_specs=pl.BlockSpec(memory_space=pl.ANY),
        scratch_shapes=[
            pltpu.VMEM((2, TM, XS_W), F32),
            pltpu.SemaphoreType.DMA((2,)),
        ],
    )
    return pl.pallas_call(
        _permute_kernel,
        grid_spec=grid_spec,
        out_shape=jax.ShapeDtypeStruct((MOE_NROWS, XS_W), F32),
        input_output_aliases={3: 0},
        compiler_params=_cparams("arbitrary"),
        name="moe_permute",
    )(dest, h2, wab, jnp.zeros((MOE_NROWS, XS_W), F32))


def _start_row_gather(src_hbm, idx_ref, base, buf, sem, n_rows):
    def body(r, carry):
        tok = idx_ref[base + r]
        pltpu.make_async_copy(src_hbm.at[pl.ds(tok, 1)], buf.at[pl.ds(r, 1)], sem).start()
        return carry

    lax.fori_loop(0, n_rows, body, 0)


def _wait_row_gather(src_hbm, buf, sem, n_rows):
    pltpu.make_async_copy(src_hbm.at[pl.ds(0, n_rows)], buf, sem).wait()


def _moe_kernel(ea_ref, eb_ref, nused_ref, xs_ref, wga_ref, wua_ref, wda_ref, wgb_ref, wub_ref, wdb_ref, o_ref):
    i = pl.program_id(0)
    n_used = nused_ref[0]

    @pl.when(i < n_used)
    def _():
        xb = xs_ref[:, 0:D_MODEL].astype(BF16)

        def expert(wg_ref, wu_ref, wd_ref):
            g = jnp.dot(xb, wg_ref[0, 0], preferred_element_type=F32)
            u = jnp.dot(xb, wu_ref[0, 0], preferred_element_type=F32)
            act = (jax.nn.silu(g) * u).astype(BF16)
            return jnp.dot(act, wd_ref[0, 0], preferred_element_type=F32)

        wa = xs_ref[:, D_MODEL:D_MODEL + 1]
        wb = xs_ref[:, D_MODEL + 1:D_MODEL + 2]
        o_ref[...] = wa * expert(wga_ref, wua_ref, wda_ref) + wb * expert(wgb_ref, wub_ref, wdb_ref)

    @pl.when(i >= n_used)
    def _():
        o_ref[...] = jnp.zeros_like(o_ref)


def _moe(xs, block_ea, block_eb, n_used, wg_bf, wu_bf, wd_bf, l):
    up = lambda sel: pl.BlockSpec((1, 1, D_MODEL, D_FF_E), lambda i, ea, eb, nu: (l, (ea, eb)[sel][i], 0, 0))
    down = lambda sel: pl.BlockSpec((1, 1, D_FF_E, D_MODEL), lambda i, ea, eb, nu: (l, (ea, eb)[sel][i], 0, 0))
    grid_spec = pltpu.PrefetchScalarGridSpec(
        num_scalar_prefetch=3,
        grid=(MOE_BLOCKS,),
        in_specs=[
            pl.BlockSpec((MOE_ROWS, XS_W), lambda i, ea, eb, nu: (i, 0)),
            up(0), up(0), down(0), up(1), up(1), down(1),
        ],
        out_specs=pl.BlockSpec((MOE_ROWS, D_MODEL), lambda i, ea, eb, nu: (i, 0)),
    )
    return pl.pallas_call(
        _moe_kernel,
        grid_spec=grid_spec,
        out_shape=jax.ShapeDtypeStruct((MOE_NROWS, D_MODEL), F32),
        compiler_params=_cparams("arbitrary"),
        name="moe_experts",
    )(block_ea, block_eb, n_used, xs, wg_bf, wu_bf, wd_bf, wg_bf, wu_bf, wd_bf)


FIN_TM = 128
FIN_TILES = T_ALL // FIN_TM


def _final_kernel(pos_ref, y_hbm, x1_ref, mod_ref, g_ref, b_ref, o_ref, ybuf, sems):
    i = pl.program_id(0)
    slot = i % 2

    @pl.when(i == 0)
    def _():
        _start_row_gather(y_hbm, pos_ref, 0, ybuf.at[0], sems.at[0], FIN_TM)

    @pl.when(i + 1 < FIN_TILES)
    def _():
        _start_row_gather(y_hbm, pos_ref, (i + 1) * FIN_TM, ybuf.at[1 - slot], sems.at[1 - slot], FIN_TM)

    _wait_row_gather(y_hbm, ybuf.at[slot], sems.at[slot], FIN_TM)
    gate = mod_ref[0, 0, 5:6, :]
    o_ref[...] = _layer_norm_rows(ALPHA * x1_ref[...] + gate * ybuf[slot], g_ref[0, 1:2, :], b_ref[0, 1:2, :])


def _final(y_sorted, pos, x1, mods, ln_g, ln_b, l):
    per_group = TM // FIN_TM
    grid_spec = pltpu.PrefetchScalarGridSpec(
        num_scalar_prefetch=1,
        grid=(FIN_TILES,),
        in_specs=[
            pl.BlockSpec(memory_space=pl.ANY),
            pl.BlockSpec((FIN_TM, D_MODEL), lambda i, pos: (i, 0)),
            pl.BlockSpec((1, 1, 6, D_MODEL), lambda i, pos: (l, _tile_group(i // per_group), 0, 0)),
            pl.BlockSpec((1, 2, D_MODEL), lambda i, pos: (l, 0, 0)),
            pl.BlockSpec((1, 2, D_MODEL), lambda i, pos: (l, 0, 0)),
        ],
        out_specs=pl.BlockSpec((FIN_TM, D_MODEL), lambda i, pos: (i, 0)),
        scratch_shapes=[
            pltpu.VMEM((2, FIN_TM, D_MODEL), F32),
            pltpu.SemaphoreType.DMA((2,)),
        ],
    )
    return pl.pallas_call(
        _final_kernel,
        grid_spec=grid_spec,
        out_shape=jax.ShapeDtypeStruct((T_ALL, D_MODEL), F32),
        compiler_params=_cparams("arbitrary"),
        name="moe_combine_ln",
    )(pos, y_sorted, x1, mods, ln_g, ln_b)


def kernel(x_prompt, x_sample, c, cache_k, cache_v, state_hgrn, c_ctx, w_ada, b_ada, w_in, w_out, diff_lambda,
           attn_norm_g, mlp_ln_g, mlp_ln_b, w_spatial, b_spatial, hgrn_lb, hgrn_norm_g, ln_g, ln_b, w_router,
           router_bias, w_gate, w_up, w_down):
    x = jnp.concatenate([x_prompt.reshape(T_CTX, D_MODEL), x_sample.reshape(T_DEC, D_MODEL)], axis=0)
    cond = jnp.concatenate([c_ctx[None, :], c, jnp.zeros((COND_PAD - N_COND, D_MODEL), F32)], axis=0)
    mods = _ada_modulation(cond, w_ada, b_ada)

    w_in_bf, w_out_bf = w_in.astype(BF16), w_out.astype(BF16)
    wg_bf, wu_bf, wd_bf = w_gate.astype(BF16), w_up.astype(BF16), w_down.astype(BF16)
    cos_t, sin_t = _rope_tables()
    cache_k4 = cache_k.reshape(DEC_BATCH, DEPTH, PAST_LEN, QK_W_A)
    cache_v4 = cache_v.reshape(DEC_BATCH, DEPTH, PAST_LEN, W_A)
    b_spatial_t = jnp.swapaxes(b_spatial, 1, 2)
    w_router_t = w_router.T

    p = jax.nn.softmax(hgrn_lb.astype(F32), axis=1)
    cs = jnp.cumsum(p, axis=1)
    lb_all = (cs - cs[:, :1]).reshape(2, DEPTH, 1, WK_C)

    new_k, new_v, new_s = [], [], []
    for l in range(DEPTH):
        proj = _inproj(x, mods, w_in_bf, cos_t, sin_t, l)
        o_a = jnp.concatenate([
            _attention(proj, None, None, diff_lambda, attn_norm_g, l, decode=False),
            _attention(proj, cache_k4, cache_v4, diff_lambda, attn_norm_g, l, decode=True)], axis=0)
        o_b = _gmlp(proj, mlp_ln_g, mlp_ln_b, w_spatial, b_spatial_t, l)
        s0 = [jnp.concatenate([jnp.zeros((1, W_C, WK_C), F32), _state_to_blockdiag(state_hgrn[:, l, d])], axis=0)
              for d in range(2)]
        o_f, s_f = _hgrn_direction(proj, lb_all, s0[0], l, 0)
        o_c, s_b = _hgrn_direction(proj, lb_all, s0[1], l, 1, o_fwd=o_f, hgrn_norm_g=hgrn_norm_g)
        x1, h2, logits_t = _outproj(o_a, o_b, o_c, x, mods, w_out_bf, ln_g, ln_b, w_router_t, l)
        dest, wab, block_ea, block_eb, n_used = _route(logits_t, router_bias)
        xs = _permute(h2, dest, wab)
        y_sorted = _moe(xs, block_ea, block_eb, n_used, wg_bf, wu_bf, wd_bf, l)
        x = _final(y_sorted, dest, x1, mods, ln_g, ln_b, l)

        new_k.append(proj[:T_CTX, OFF_KA:OFF_KA + QK_W_A].reshape(BATCH, SEQ, H_A, 2, DQK_A))
        new_v.append(proj[:T_CTX, OFF_VA:OFF_VA + W_A].reshape(BATCH, SEQ, H_A, DV_A))
        new_s.append(jnp.stack([_blockdiag_to_state(s_f[:BATCH]), _blockdiag_to_state(s_b[:BATCH])], axis=1))

    return (x[:T_CTX].reshape(BATCH, SEQ, D_MODEL), x[T_CTX:].reshape(DEC_BATCH, DEC_SEQ, D_MODEL),
            jnp.stack(new_k, axis=1), jnp.stack(new_v, axis=1), jnp.stack(new_s, axis=1))
```

```python
import functools
import math

import jax
import jax.numpy as jnp
from jax import lax
from jax.experimental import pallas as pl
from jax.experimental.pallas import tpu as pltpu

D_MODEL = 1024
BATCH = 16
SEQ = 256
DEPTH = 4
DEC_BATCH = 4
DEC_SEQ = 4096
PAST_LEN = 512
GRID_W = 64
H_A = 4
DQK_A = 64
DV_A = 128
QK_W_A = H_A * 2 * DQK_A
W_A = H_A * DV_A
G_B = 4
DG_B = 64
W_B = G_B * DG_B
CHUNK_B = 128
H_C = 4
DK_C = 64
DV_C = 64
WK_C = H_C * DK_C
W_C = H_C * DV_C
MIX_W = W_A + W_B + W_C
OFF_QA = 0
OFF_KA = OFF_QA + QK_W_A
OFF_VA = OFF_KA + QK_W_A
OFF_UB = OFF_VA + W_A
OFF_VB = OFF_UB + W_B
OFF_IC = OFF_VB + W_B
OFF_FC = OFF_IC + W_C
OFF_QC = OFF_FC + 2 * WK_C
OFF_GC = OFF_QC + 2 * WK_C
N_IN = OFF_GC + W_C
ROPE_THETA = 10000.0
N_EXP = 16
N_GROUP = 4
EXP_PER_GROUP = N_EXP // N_GROUP
TOP_K = 2
GROUP_TOP = 2
D_FF_E = 512
LN_EPS = 1e-5
ALPHA = (2 * DEPTH) ** 0.25

T_CTX = BATCH * SEQ
T_DEC = DEC_BATCH * DEC_SEQ
T_ALL = T_CTX + T_DEC
N_COND = 1 + DEC_BATCH
COND_PAD = 8

LANES = 128
TM = 256
CTX_TILES = T_CTX // TM
DEC_TILES_PER_SEQ = DEC_SEQ // TM
DEC_TILES = T_DEC // TM
N_TILES = T_ALL // TM
assert SEQ == TM and CTX_TILES == DEC_TILES_PER_SEQ and DEC_SEQ == T_CTX
HB = 16
MOE_ROWS = 128
N_PAIR = N_GROUP * (EXP_PER_GROUP * (EXP_PER_GROUP - 1) // 2)
MOE_BLOCKS = T_ALL // MOE_ROWS + N_PAIR
MOE_NROWS = MOE_BLOCKS * MOE_ROWS
VMEM_LIMIT = 56 * 1024 * 1024

BF16 = jnp.bfloat16
F32 = jnp.float32
HIGHEST = lax.Precision.HIGHEST
NT_DIMS = (((1,), (1,)), ((), ()))
TN_DIMS = (((0,), (0,)), ((), ()))


def _cparams(*sem):
    return pltpu.CompilerParams(dimension_semantics=sem, vmem_limit_bytes=VMEM_LIMIT)


def _tile_group(i):
    return jnp.where(i < DEC_TILES, 1 + i // DEC_TILES_PER_SEQ, 0)


def _layer_norm_rows(z, g, b):
    mu = jnp.mean(z, axis=-1, keepdims=True)
    zc = z - mu
    var = jnp.mean(zc * zc, axis=-1, keepdims=True)
    return zc * lax.rsqrt(var + LN_EPS) * g + b


ADA_TN = 1536


def _ada_kernel(cond_ref, w_ref, b_ref, o_ref):
    a = jax.nn.silu(cond_ref[...]).astype(BF16)
    o_ref[0] = jnp.dot(a, w_ref[0].astype(BF16), preferred_element_type=F32) + b_ref[0]


def _ada_modulation(cond, w_ada, b_ada):
    n = 6 * D_MODEL
    out = pl.pallas_call(
        _ada_kernel,
        grid=(DEPTH, n // ADA_TN),
        in_specs=[
            pl.BlockSpec((COND_PAD, D_MODEL), lambda l, j: (0, 0)),
            pl.BlockSpec((1, D_MODEL, ADA_TN), lambda l, j: (l, 0, j)),
            pl.BlockSpec((1, 1, ADA_TN), lambda l, j: (l, 0, j)),
        ],
        out_specs=pl.BlockSpec((1, COND_PAD, ADA_TN), lambda l, j: (l, 0, j)),
        out_shape=jax.ShapeDtypeStruct((DEPTH, COND_PAD, n), F32),
        compiler_params=_cparams("parallel", "parallel"),
        name="ada_modulation",
    )(cond, w_ada, b_ada.reshape(DEPTH, 1, n))
    return out.reshape(DEPTH, COND_PAD, 6, D_MODEL)


INPROJ_TN = 512


def _rope_tables():
    pos = jnp.arange(DEC_SEQ)
    row = (pos // GRID_W).astype(F32)
    col = (pos % GRID_W).astype(F32)
    quarter = DQK_A // 4
    inv = ROPE_THETA ** (-jnp.arange(quarter, dtype=F32) / quarter)
    lane = jnp.arange(LANES)
    in_head = lane % DQK_A
    use_col = in_head >= DQK_A // 2
    freq = inv[lane % quarter]
    ang = jnp.where(use_col[None, :], col[:, None], row[:, None]) * freq[None, :]
    sign = jnp.where((lane % (2 * quarter)) < quarter, -1.0, 1.0).astype(F32)
    cos = jnp.concatenate([jnp.cos(ang), jnp.ones((TM, LANES), F32)], axis=0)
    sin = jnp.concatenate([jnp.sin(ang) * sign[None, :], jnp.zeros((TM, LANES), F32)], axis=0)
    return cos, sin


def _inproj_kernel(x_ref, mod_ref, w_ref, cos_ref, sin_ref, o_ref):
    shift = mod_ref[0, 0, 0:1, :]
    scale = mod_ref[0, 0, 1:2, :]
    h = (x_ref[...] * (1.0 + scale) + shift).astype(BF16)
    cos = cos_ref[...]
    sin = sin_ref[...]
    lane = lax.broadcasted_iota(jnp.int32, (TM, LANES), 1)
    first = (lane % (DQK_A // 2)) < (DQK_A // 4)
    for j in range(N_IN // INPROJ_TN):
        acc = jnp.dot(h, w_ref[0, :, j * INPROJ_TN:(j + 1) * INPROJ_TN], preferred_element_type=F32)
        if (j + 1) * INPROJ_TN <= OFF_VA:
            for cb in range(INPROJ_TN // LANES):
                blk = acc[:, cb * LANES:(cb + 1) * LANES]
                partner = jnp.where(first, pltpu.roll(blk, LANES - DQK_A // 4, 1), pltpu.roll(blk, DQK_A // 4, 1))
                o_ref[:, j * INPROJ_TN + cb * LANES:j * INPROJ_TN + (cb + 1) * LANES] = blk * cos + partner * sin
        else:
            o_ref[:, j * INPROJ_TN:(j + 1) * INPROJ_TN] = acc


def _inproj(x, mods, w_in_bf, cos_t, sin_t, l):
    def rope_blk(i):
        return jnp.where(i < DEC_TILES, i % DEC_TILES_PER_SEQ, DEC_TILES_PER_SEQ)

    return pl.pallas_call(
        _inproj_kernel,
        grid=(N_TILES,),
        in_specs=[
            pl.BlockSpec((TM, D_MODEL), lambda i: (i, 0)),
            pl.BlockSpec((1, 1, 6, D_MODEL), lambda i: (l, _tile_group(i), 0, 0)),
            pl.BlockSpec((1, D_MODEL, N_IN), lambda i: (l, 0, 0)),
            pl.BlockSpec((TM, LANES), lambda i: (rope_blk(i), 0)),
            pl.BlockSpec((TM, LANES), lambda i: (rope_blk(i), 0)),
        ],
        out_specs=pl.BlockSpec((TM, N_IN), lambda i: (i, 0)),
        out_shape=jax.ShapeDtypeStruct((T_ALL, N_IN), F32),
        compiler_params=_cparams("arbitrary"),
        name="inproj",
    )(x, mods, w_in_bf, cos_t, sin_t)


ATTN_KC = 512


def _attn_kernel(*refs, lam_init, decode):
    if decode:
        q_ref, kn_ref, vn_ref, kc_ref, vc_ref, dl_ref, g_ref, o_ref, kb_ref, vb_ref, s_ref = refs
    else:
        q_ref, kn_ref, vn_ref, dl_ref, g_ref, o_ref, kb_ref, vb_ref, s_ref = refs
    n_new = kn_ref.shape[0]
    n_keys = kb_ref.shape[0]
    kc = min(ATTN_KC, n_keys)
    chunks = [slice(c * kc, (c + 1) * kc) for c in range(n_keys // kc)]

    @pl.when(pl.program_id(2) == 0)
    def _():
        kb_ref[0:n_new, :] = kn_ref[...].astype(BF16)
        for c in range(n_new // kc):
            vb_ref[:, chunks[c]] = vn_ref[chunks[c], :].T.astype(BF16)
        if decode:
            kb_ref[n_new:n_keys, :] = kc_ref[0, 0].astype(BF16)
            vb_ref[:, n_new:n_keys] = vc_ref[0, 0].T.astype(BF16)

    dl = dl_ref[0]
    lam = (jnp.exp(jnp.sum(dl[0:1] * dl[1:2], axis=-1, keepdims=True))
           - jnp.exp(jnp.sum(dl[2:3] * dl[3:4], axis=-1, keepdims=True)) + lam_init)
    q = q_ref[...] * (DQK_A ** -0.5 * math.log2(math.e))
    n_q = q.shape[0]
    lane = lax.broadcasted_iota(jnp.int32, q.shape, 1)
    fold = lambda a, op: op(a.reshape(a.shape[0] // 8, 8, a.shape[1]), axis=0)
    qm = [jnp.where((lane // DQK_A) == m, q, 0.0).astype(BF16) for m in range(2)]

    def score_chunk(m, ch, run_max):
        s = lax.dot_general(kb_ref[ch, :], qm[m], NT_DIMS, preferred_element_type=F32)
        s_ref[m, ch, :] = s
        cmax = fold(s, jnp.max)
        return cmax if run_max is None else jnp.maximum(run_max, cmax)

    def value_chunk(m, ch, mx, run_sum, acc):
        p = jnp.exp2(s_ref[m, ch, :] - mx)
        run_sum = run_sum + fold(p, jnp.sum)
        return run_sum, acc + jnp.dot(vb_ref[:, ch], p.astype(BF16), preferred_element_type=F32)

    zeros = lambda: (jnp.zeros((8, n_q), F32), jnp.zeros((DV_A, n_q), F32))
    run_max0 = run_max1 = None
    for ch in chunks:
        run_max0 = score_chunk(0, ch, run_max0)
    mx0 = jnp.max(run_max0, axis=0, keepdims=True)
    sum0, acc0 = zeros()
    for ch in chunks:
        run_max1 = score_chunk(1, ch, run_max1)
        sum0, acc0 = value_chunk(0, ch, mx0, sum0, acc0)
    mx1 = jnp.max(run_max1, axis=0, keepdims=True)
    sum1, acc1 = zeros()
    for ch in chunks:
        sum1, acc1 = value_chunk(1, ch, mx1, sum1, acc1)
    outs = [acc0 / jnp.sum(sum0, axis=0, keepdims=True), acc1 / jnp.sum(sum1, axis=0, keepdims=True)]
    o = outs[0] - lam * outs[1]
    ms = jnp.mean(o * o, axis=0, keepdims=True)
    o_ref[...] = (o * lax.rsqrt(ms + LN_EPS) * g_ref[0] * (1.0 - lam_init)).T


def _attention(proj, cache_k4, cache_v4, diff_lambda, attn_norm_g, l, decode):
    lam_init = 0.8 - 0.6 * math.exp(-0.3 * l)
    qcol, kcol, vcol = OFF_QA // LANES, OFF_KA // LANES, OFF_VA // LANES
    common = [
        pl.BlockSpec((1, 4, DQK_A), lambda b, h, i: (l, 0, 0)),
        pl.BlockSpec((1, DV_A, 1), lambda b, h, i: (l, 0, 0)),
    ]
    if decode:
        n_keys = DEC_SEQ + PAST_LEN
        grid = (DEC_BATCH, H_A, DEC_TILES_PER_SEQ)
        q_spec = pl.BlockSpec((TM, LANES), lambda b, h, i: (b * DEC_TILES_PER_SEQ + i, qcol + h))
        in_specs = [
            q_spec,
            pl.BlockSpec((DEC_SEQ, LANES), lambda b, h, i: (b, kcol + h)),
            pl.BlockSpec((DEC_SEQ, LANES), lambda b, h, i: (b, vcol + h)),
            pl.BlockSpec((1, 1, PAST_LEN, LANES), lambda b, h, i: (b, l, 0, h)),
            pl.BlockSpec((1, 1, PAST_LEN, LANES), lambda b, h, i: (b, l, 0, h)),
        ] + common
        args = (proj, proj, proj, cache_k4, cache_v4, diff_lambda, attn_norm_g.reshape(DEPTH, DV_A, 1))
        out_spec = pl.BlockSpec((TM, LANES), lambda b, h, i: (b * DEC_TILES_PER_SEQ + i, h))
        out_rows = T_DEC
    else:
        n_keys = SEQ
        grid = (BATCH, H_A, 1)
        in_specs = [
            pl.BlockSpec((SEQ, LANES), lambda b, h, i: (DEC_TILES + b, qcol + h)),
            pl.BlockSpec((SEQ, LANES), lambda b, h, i: (DEC_TILES + b, kcol + h)),
            pl.BlockSpec((SEQ, LANES), lambda b, h, i: (DEC_TILES + b, vcol + h)),
        ] + common
        args = (proj, proj, proj, diff_lambda, attn_norm_g.reshape(DEPTH, DV_A, 1))
        out_spec = pl.BlockSpec((SEQ, LANES), lambda b, h, i: (b, h))
        out_rows = T_CTX
    return pl.pallas_call(
        functools.partial(_attn_kernel, lam_init=lam_init, decode=decode),
        grid=grid,
        in_specs=in_specs,
        out_specs=out_spec,
        out_shape=jax.ShapeDtypeStruct((out_rows, W_A), F32),
        scratch_shapes=[
            pltpu.VMEM((n_keys, LANES), BF16),
            pltpu.VMEM((DV_A, n_keys), BF16),
            pltpu.VMEM((2, n_keys, TM), F32),
        ],
        compiler_params=_cparams("parallel", "parallel", "arbitrary"),
        name="attn_dec" if decode else "attn_ctx",
    )(*args)


GMLP_TM = 512


def _gmlp_kernel(u_ref, v_ref, g_ref, b_ref, ws_ref, bs_ref, o_ref):
    lane = lax.broadcasted_iota(jnp.int32, (CHUNK_B, W_B), 1)
    masks = [(lane // DG_B) == g for g in range(G_B)]
    bias = functools.reduce(
        jnp.add, [jnp.where(masks[g], bs_ref[0, :, g:g + 1], 0.0) for g in range(G_B)])
    for c in range(GMLP_TM // CHUNK_B):
        rows = slice(c * CHUNK_B, (c + 1) * CHUNK_B)
        u = jax.nn.gelu(u_ref[rows, :])
        v = _layer_norm_rows(jax.nn.gelu(v_ref[rows, :]), g_ref[0], b_ref[0])
        mixed = bias
        for g in range(G_B):
            vg = jnp.where(masks[g], v, 0.0).astype(BF16)
            mixed = mixed + jnp.dot(ws_ref[0, g].astype(BF16), vg, preferred_element_type=F32)
        o_ref[rows, :] = u * mixed


def _gmlp(proj, mlp_ln_g, mlp_ln_b, w_spatial, b_spatial_t, l):
    return pl.pallas_call(
        _gmlp_kernel,
        grid=(T_ALL // GMLP_TM,),
        in_specs=[
            pl.BlockSpec((GMLP_TM, W_B), lambda i: (i, OFF_UB // W_B)),
            pl.BlockSpec((GMLP_TM, W_B), lambda i: (i, OFF_VB // W_B)),
            pl.BlockSpec((1, 1, W_B), lambda i: (l, 0, 0)),
            pl.BlockSpec((1, 1, W_B), lambda i: (l, 0, 0)),
            pl.BlockSpec((1, G_B, CHUNK_B, CHUNK_B), lambda i: (l, 0, 0, 0)),
            pl.BlockSpec((1, CHUNK_B, G_B), lambda i: (l, 0, 0)),
        ],
        out_specs=pl.BlockSpec((GMLP_TM, W_B), lambda i: (i, 0)),
        out_shape=jax.ShapeDtypeStruct((T_ALL, W_B), F32),
        compiler_params=_cparams("parallel"),
        name="gmlp",
    )(proj, proj, mlp_ln_g.reshape(DEPTH, 1, W_B), mlp_ln_b.reshape(DEPTH, 1, W_B), w_spatial, b_spatial_t)


HG = 4
N_SLAB = DEC_BATCH + 1
assert HG == DEC_BATCH and BATCH % HG == 0


def _hgrn_kernel(*refs, backward, decode):
    refs = list(refs)
    i_ref, f_ref, q_ref, lb_ref = refs[:4]
    del refs[:4]
    s0_ref = refs.pop(0) if decode else None
    if backward:
        of_ref, gate_ref, g_ref = refs[:3]
        del refs[:3]
    o_ref = refs.pop(0)
    sfin_ref = None if decode else refs.pop(0)
    st_ref, = refs
    tile = (lambda ref, s: ref.at[s, 0]) if decode else (lambda ref, s: ref.at[0, s])

    if decode:
        @pl.when(pl.program_id(0) == 0)
        def _():
            st_ref[...] = s0_ref[...]
    else:
        st_ref[...] = jnp.zeros_like(st_ref)

    lb = lb_ref[0, 0]
    r = lax.broadcasted_iota(jnp.int32, (HB, HB), 0)
    c = lax.broadcasted_iota(jnp.int32, (HB, HB), 1)
    seen = (c >= r) if backward else (c <= r)
    tri = seen.astype(F32)
    lane = lax.broadcasted_iota(jnp.int32, (HB, WK_C), 1)
    head_masks = [(lane // DK_C) == h for h in range(H_C)]
    rr = lax.broadcasted_iota(jnp.int32, (W_C, WK_C), 0)
    cc = lax.broadcasted_iota(jnp.int32, (W_C, WK_C), 1)
    same_head = (rr // DV_C) == (cc // DK_C)
    n_blk = TM // HB

    sr = lax.broadcasted_iota(jnp.int32, (HB, H_C * HB), 0)
    tc = lax.broadcasted_iota(jnp.int32, (HB, H_C * HB), 1) % HB
    seen_t = (sr >= tc) if backward else (sr <= tc)
    seqs = range(HG)

    def advance(it, carry):
        blk = (n_blk - 1 - it) if backward else it
        rows = pl.ds(pl.multiple_of(blk * HB, HB), HB)
        f = [lb + (1.0 - lb) * jax.nn.sigmoid(tile(f_ref, s)[rows, :]) for s in seqs]
        kk = [1.0 - f[s] for s in seqs]
        qq = [tile(q_ref, s)[rows, :] for s in seqs]
        ii = [tile(i_ref, s)[rows, :].astype(BF16) for s in seqs]
        logf = jnp.concatenate([jnp.log(f[s]) for s in seqs], axis=1)
        b_all = jnp.dot(tri, logf, precision=HIGHEST, preferred_element_type=F32)
        b = [b_all[:, s * WK_C:(s + 1) * WK_C] for s in seqs]
        b_end = [bs[0:1, :] if backward else bs[HB - 1:HB, :] for bs in b]
        qe = [qq[s] * jnp.exp(b[s]) for s in seqs]
        ke = [(kk[s] * jnp.exp(-b[s])).astype(BF16) for s in seqs]
        ks = [(kk[s] * jnp.exp(b_end[s] - b[s])).astype(BF16) for s in seqs]
        st = [st_ref[s] for s in seqs]
        o = [lax.dot_general(qe[s].astype(BF16), st[s].astype(BF16), NT_DIMS, preferred_element_type=F32)
             for s in seqs]
        q_heads = [jnp.concatenate([jnp.where(head_masks[h], qe[s], 0.0) for h in range(H_C)], axis=0).astype(BF16)
                   for s in seqs]
        at = [lax.dot_general(ke[s], q_heads[s], NT_DIMS, preferred_element_type=F32) for s in seqs]
        at = [jnp.where(seen_t, a, 0.0).astype(BF16) for a in at]
        intra = [lax.dot_general(at[s], ii[s], TN_DIMS, preferred_element_type=F32) for s in seqs]
        upd = [lax.dot_general(ii[s], ks[s], TN_DIMS, preferred_element_type=F32) for s in seqs]
        for s in seqs:
            st_ref[s] = st[s] * jnp.exp(b_end[s]) + jnp.where(same_head, upd[s], 0.0)
            out = o[s]
            for h in range(H_C):
                out = out + jnp.where(head_masks[h], intra[s][h * HB:(h + 1) * HB, :], 0.0)
            if backward:
                out = out + tile(of_ref, s)[rows, :]
            tile(o_ref, s)[rows, :] = out
        return carry

    lax.fori_loop(0, n_blk, advance, 0)
    if not decode:
        sfin_ref[...] = st_ref[...]
    if backward:
        r2 = lax.broadcasted_iota(jnp.int32, (W_C, W_C), 0)
        c2 = lax.broadcasted_iota(jnp.int32, (W_C, W_C), 1)
        head_mean = jnp.where((r2 // DV_C) == (c2 // DV_C), 1.0 / DV_C, 0.0).astype(F32)
        for s in range(HG):
            tot = tile(o_ref, s)[...]
            ms = jnp.dot(tot * tot, head_mean, precision=HIGHEST, preferred_element_type=F32)
            tile(o_ref, s)[...] = tot * lax.rsqrt(ms + LN_EPS) * g_ref[0] * jax.nn.silu(tile(gate_ref, s)[...])


def _hgrn_call(proj4, lb_all, l, d, decode, s0=None, o_fwd=None, hgrn_norm_g=None):
    backward = d == 1
    col = lambda off: off // W_C
    if decode:
        n_steps, n_slab, slab0 = DEC_TILES_PER_SEQ, DEC_BATCH, 0
        blk_shape = (HG, 1, TM, W_C)
        pos = (lambda j, slab: (slab, n_steps - 1 - j, 0)) if backward else (lambda j, slab: (slab, j, 0))
    else:
        n_steps, n_slab, slab0 = BATCH // HG, 1, DEC_BATCH
        blk_shape = (1, HG, TM, W_C)
        pos = lambda j, slab: (slab, j, 0)
    proj_spec = lambda cidx: pl.BlockSpec(blk_shape, lambda j: pos(j, slab0) + (cidx,))
    own_spec = pl.BlockSpec(blk_shape, lambda j: pos(j, 0) + (0,))
    in_specs = [
        proj_spec(col(OFF_IC)),
        proj_spec(col(OFF_FC) + d),
        proj_spec(col(OFF_QC) + d),
        pl.BlockSpec((1, 1, 1, WK_C), lambda j: (d, l, 0, 0)),
    ]
    args = [proj4, proj4, proj4, lb_all]
    if decode:
        in_specs.append(pl.BlockSpec((HG, W_C, WK_C), lambda j: (0, 0, 0)))
        args.append(s0)
    if backward:
        in_specs += [own_spec, proj_spec(col(OFF_GC)), pl.BlockSpec((1, 1, W_C), lambda j: (l, 0, 0))]
        args += [o_fwd, proj4, hgrn_norm_g.reshape(DEPTH, 1, W_C)]
    out_specs = [own_spec]
    out_shape = [jax.ShapeDtypeStruct((n_slab, CTX_TILES, TM, W_C), F32)]
    if not decode:
        out_specs.append(pl.BlockSpec((HG, W_C, WK_C), lambda j: (j, 0, 0)))
        out_shape.append(jax.ShapeDtypeStruct((BATCH, W_C, WK_C), F32))
    return pl.pallas_call(
        functools.partial(_hgrn_kernel, backward=backward, decode=decode),
        grid=(n_steps,),
        in_specs=in_specs,
        out_specs=out_specs,
        out_shape=out_shape,
        scratch_shapes=[pltpu.VMEM((HG, W_C, WK_C), F32)],
        compiler_params=_cparams("arbitrary"),
        name=("hgrn_dec" if decode else "hgrn_ctx") + ("_bwd" if backward else "_fwd"),
    )(*args)


def _hgrn(proj, lb_all, state_l, hgrn_norm_g, l):
    proj4 = proj.reshape(N_SLAB, CTX_TILES, TM, N_IN)
    s0 = [_state_to_blockdiag(state_l[:, d]) for d in range(2)]
    f_dec, = _hgrn_call(proj4, lb_all, l, 0, True, s0=s0[0])
    f_ctx, s_f = _hgrn_call(proj4, lb_all, l, 0, False)
    o_dec, = _hgrn_call(proj4, lb_all, l, 1, True, s0=s0[1], o_fwd=f_dec, hgrn_norm_g=hgrn_norm_g)
    o_ctx, s_b = _hgrn_call(proj4, lb_all, l, 1, False, o_fwd=f_ctx, hgrn_norm_g=hgrn_norm_g)
    states = jnp.stack([_blockdiag_to_state(s_f), _blockdiag_to_state(s_b)], axis=1)
    return o_dec.reshape(T_DEC, W_C), o_ctx.reshape(T_CTX, W_C), states


def _state_to_blockdiag(s):
    st = jnp.swapaxes(s, -1, -2)
    eye = jnp.eye(H_C, dtype=s.dtype)
    return jnp.einsum('nhed,hg->nhegd', st, eye).reshape(s.shape[0], W_C, WK_C)


def _blockdiag_to_state(st):
    s5 = st.reshape(st.shape[0], H_C, DV_C, H_C, DK_C)
    diag = jnp.stack([s5[:, h, :, h, :] for h in range(H_C)], axis=1)
    return jnp.swapaxes(diag, -1, -2)


def _outproj_kernel(oa_dec_ref, oa_ctx_ref, ob_ref, oc_dec_ref, oc_ctx_ref, x_ref, mod_ref, w_ref, g_ref, b_ref,
                    wr_ref, x1_ref, h2_ref, lg_ref):
    is_dec = pl.program_id(0) < DEC_TILES
    o_a = jnp.where(is_dec, oa_dec_ref[...], oa_ctx_ref[...])
    o_c = jnp.where(is_dec, oc_dec_ref[...], oc_ctx_ref[...])
    mixed = jnp.concatenate([o_a, ob_ref[...], o_c], axis=1).astype(BF16)
    y = jnp.dot(mixed, w_ref[0], preferred_element_type=F32)
    gate = mod_ref[0, 0, 2:3, :]
    x1 = _layer_norm_rows(ALPHA * x_ref[...] + gate * y, g_ref[0, 0:1, :], b_ref[0, 0:1, :])
    h2 = x1 * (1.0 + mod_ref[0, 0, 4:5, :]) + mod_ref[0, 0, 3:4, :]
    x1_ref[...] = x1
    h2_ref[...] = h2
    lg_ref[0] = lax.dot_general(wr_ref[...], h2, NT_DIMS, precision=HIGHEST, preferred_element_type=F32)


def _outproj(oa_dec, oa_ctx, o_b, oc_dec, oc_ctx, x, mods, w_out_bf, ln_g, ln_b, w_router_t, l):
    row = lambda w: pl.BlockSpec((TM, w), lambda i: (i, 0))
    dec_row = lambda w: pl.BlockSpec((TM, w), lambda i: (jnp.minimum(i, DEC_TILES - 1), 0))
    ctx_row = lambda w: pl.BlockSpec((TM, w), lambda i: (jnp.maximum(i - DEC_TILES, 0), 0))
    return pl.pallas_call(
        _outproj_kernel,
        grid=(N_TILES,),
        in_specs=[
            dec_row(W_A), ctx_row(W_A), row(W_B), dec_row(W_C), ctx_row(W_C), row(D_MODEL),
            pl.BlockSpec((1, 1, 6, D_MODEL), lambda i: (l, _tile_group(i), 0, 0)),
            pl.BlockSpec((1, MIX_W, D_MODEL), lambda i: (l, 0, 0)),
            pl.BlockSpec((1, 2, D_MODEL), lambda i: (l, 0, 0)),
            pl.BlockSpec((1, 2, D_MODEL), lambda i: (l, 0, 0)),
            pl.BlockSpec((N_EXP, D_MODEL), lambda i: (0, 0)),
        ],
        out_specs=[row(D_MODEL), row(D_MODEL), pl.BlockSpec((1, N_EXP, TM), lambda i: (i, 0, 0))],
        out_shape=[
            jax.ShapeDtypeStruct((T_ALL, D_MODEL), F32),
            jax.ShapeDtypeStruct((T_ALL, D_MODEL), F32),
            jax.ShapeDtypeStruct((N_TILES, N_EXP, TM), F32),
        ],
        compiler_params=_cparams("parallel"),
        name="outproj",
    )(oa_dec, oa_ctx, o_b, oc_dec, oc_ctx, x, mods, w_out_bf, ln_g, ln_b, w_router_t)


KEY_PAD = 32
PAIRS_PER_GROUP = N_PAIR // N_GROUP
assert EXP_PER_GROUP == 4 and TOP_K == 2 and GROUP_TOP == 2 and MOE_BLOCKS <= TM


def _pair_index(la, lb):
    return jnp.right_shift(la * (2 * EXP_PER_GROUP - 1 - la), 1) + (lb - la - 1)


def _route_kernel(lg_ref, bias_ref, dest_ref, wa_ref, wb_ref, bea_ref, beb_ref, nused_ref, key_scr, rank_scr):
    bias = bias_ref[...]
    kio = lax.broadcasted_iota(jnp.int32, (KEY_PAD, TM), 0)
    tr = lax.broadcasted_iota(jnp.int32, (TM, TM), 0)
    tc = lax.broadcasted_iota(jnp.int32, (TM, TM), 1)
    upto = jnp.where(tr <= tc, 1.0, 0.0).astype(BF16)

    def select(ti, counts):
        scores = jax.nn.sigmoid(lg_ref[ti])
        sel = scores + bias
        row = lambda a, e: a[e:e + 1, :]
        gscore = []
        for g in range(N_GROUP):
            v = [row(sel, EXP_PER_GROUP * g + j) for j in range(EXP_PER_GROUP)]
            sums = [v[a] + v[b] for a in range(EXP_PER_GROUP) for b in range(a + 1, EXP_PER_GROUP)]
            gscore.append(functools.reduce(jnp.maximum, sums))
        gbest = jnp.zeros((1, TM), jnp.int32)
        best = gscore[0]
        for g in range(1, N_GROUP):
            better = gscore[g] > best
            gbest = jnp.where(better, g, gbest)
            best = jnp.where(better, gscore[g], best)
        vb, sb = [], []
        for j in range(EXP_PER_GROUP):
            vj, sj = row(sel, j), row(scores, j)
            for g in range(1, N_GROUP):
                vj = jnp.where(gbest == g, row(sel, EXP_PER_GROUP * g + j), vj)
                sj = jnp.where(gbest == g, row(scores, EXP_PER_GROUP * g + j), sj)
            vb.append(vj)
            sb.append(sj)
        one, zero = jnp.ones((1, TM), jnp.int32), jnp.zeros((1, TM), jnp.int32)
        rank = []
        for j in range(EXP_PER_GROUP):
            rj = zero
            for i in range(EXP_PER_GROUP):
                if i != j:
                    ahead = (vb[i] >= vb[j]) if i < j else (vb[i] > vb[j])
                    rj = rj + jnp.where(ahead, one, zero)
            rank.append(rj)
        pick_i = lambda r: functools.reduce(jnp.add, [jnp.where(rank[j] == r, j, 0) for j in range(EXP_PER_GROUP)])
        pick_s = lambda r: functools.reduce(jnp.add, [jnp.where(rank[j] == r, sb[j], 0.0) for j in range(EXP_PER_GROUP)])
        t1, t2, s1, s2 = pick_i(0), pick_i(1), pick_s(0), pick_s(1)
        den = s1 + s2
        swap = t1 > t2
        la, lb = jnp.minimum(t1, t2), jnp.maximum(t1, t2)
        wa_ref[ti] = jnp.where(swap, s2, s1) / den
        wb_ref[ti] = jnp.where(swap, s1, s2) / den
        key = gbest * PAIRS_PER_GROUP + _pair_index(la, lb)
        onehot = kio == key
        ahead_incl = jnp.dot(jnp.where(onehot, 1.0, 0.0).astype(BF16), upto, preferred_element_type=F32)
        key_scr[ti] = key
        rank_scr[ti] = jnp.sum(jnp.where(onehot, ahead_incl - 1.0 + counts, 0.0), axis=0, keepdims=True)
        return counts + ahead_incl[:, TM - 1:TM]

    counts = lax.fori_loop(0, N_TILES, select, jnp.zeros((KEY_PAD, TM), F32))
    pcounts = jnp.floor((counts + (MOE_ROWS - 1)) * (1.0 / MOE_ROWS)) * MOE_ROWS
    kr = lax.broadcasted_iota(jnp.int32, (KEY_PAD, KEY_PAD), 0)
    kc = lax.broadcasted_iota(jnp.int32, (KEY_PAD, KEY_PAD), 1)
    pends = jnp.dot(jnp.where(kc <= kr, 1.0, 0.0), pcounts, precision=HIGHEST, preferred_element_type=F32)
    pstarts = pends - pcounts

    def place(ti, carry):
        onehot = kio == key_scr[ti]
        dest = jnp.sum(jnp.where(onehot, pstarts, 0.0), axis=0, keepdims=True) + rank_scr[ti]
        dest_ref[ti] = dest.astype(jnp.int32)
        return carry

    lax.fori_loop(0, N_TILES, place, 0)

    start = lax.broadcasted_iota(jnp.int32, (KEY_PAD, TM), 1).astype(F32) * MOE_ROWS
    ended = jnp.where(kio < N_PAIR, jnp.where(pends <= start, 1, 0), 0)
    bkey = jnp.minimum(jnp.sum(ended, axis=0, keepdims=True), N_PAIR - 1)
    grp = functools.reduce(jnp.add, [jnp.where(bkey >= PAIRS_PER_GROUP * g, 1, 0) for g in range(1, N_GROUP)])
    loc = bkey - PAIRS_PER_GROUP * grp
    la = jnp.where(loc >= 3, 1, 0) + jnp.where(loc >= 5, 1, 0)
    lb = loc - _pair_index(la, la + 1) + la + 1
    bea_ref[...] = EXP_PER_GROUP * grp + la
    beb_ref[...] = EXP_PER_GROUP * grp + lb
    nused_ref[...] = (pends[N_PAIR - 1:N_PAIR, :] * (1.0 / MOE_ROWS)).astype(jnp.int32)


def _route(logits_t, router_bias):
    whole = lambda shape: pl.BlockSpec(shape, lambda i: (0,) * len(shape))
    tok = (N_TILES, 1, TM)
    dest, wa, wb, bea, beb, nused = pl.pallas_call(
        _route_kernel,
        grid=(1,),
        in_specs=[whole((N_TILES, N_EXP, TM)), whole((N_EXP, 1))],
        out_specs=[whole(tok), whole(tok), whole(tok), whole((1, TM)), whole((1, TM)), whole((1, TM))],
        out_shape=[
            jax.ShapeDtypeStruct(tok, jnp.int32), jax.ShapeDtypeStruct(tok, F32), jax.ShapeDtypeStruct(tok, F32),
            jax.ShapeDtypeStruct((1, TM), jnp.int32), jax.ShapeDtypeStruct((1, TM), jnp.int32),
            jax.ShapeDtypeStruct((1, TM), jnp.int32),
        ],
        scratch_shapes=[pltpu.VMEM(tok, jnp.int32), pltpu.VMEM(tok, F32)],
        compiler_params=_cparams("arbitrary"),
        name="route",
    )(logits_t, router_bias.astype(F32).reshape(N_EXP, 1))
    wab = jnp.stack([wa.reshape(T_ALL), wb.reshape(T_ALL)], axis=-1)
    return dest.reshape(T_ALL), wab, bea[0, :MOE_BLOCKS], beb[0, :MOE_BLOCKS], nused[0, :1]


XS_W = D_MODEL + LANES


def _permute_kernel(dest_ref, h_ref, wab_ref, xs_in_hbm, xs_hbm, stage, sems):
    del xs_in_hbm
    i = pl.program_id(0)
    slot = i % 2

    def wait_slot(s):
        pltpu.make_async_copy(stage.at[s], xs_hbm.at[pl.ds(0, TM)], sems.at[s]).wait()

    @pl.when(i >= 2)
    def _():
        wait_slot(slot)

    stage[slot, :, 0:D_MODEL] = h_ref[...]
    lane = lax.broadcasted_iota(jnp.int32, (TM, LANES), 1)
    w = wab_ref[...]
    stage[slot, :, D_MODEL:XS_W] = jnp.where(lane == 0, w[:, 0:1], jnp.where(lane == 1, w[:, 1:2], 0.0))

    _start_row_copies(TM, lambda r: pltpu.make_async_copy(
        stage.at[slot, pl.ds(r, 1)], xs_hbm.at[pl.ds(dest_ref[i * TM + r], 1)], sems.at[slot]))

    @pl.when(i == N_TILES - 1)
    def _():
        wait_slot(1 - slot)
        wait_slot(slot)


def _permute(h2, dest, wab):
    grid_spec = pltpu.PrefetchScalarGridSpec(
        num_scalar_prefetch=1,
        grid=(N_TILES,),
        in_specs=[
            pl.BlockSpec((TM, D_MODEL), lambda i, dest: (i, 0)),
            pl.BlockSpec((TM, 2), lambda i, dest: (i, 0)),
            pl.BlockSpec(memory_space=pl.ANY),
        ],
        out_specs=pl.BlockSpec(memory_space=pl.ANY),
        scratch_shapes=[
            pltpu.VMEM((2, TM, XS_W), F32),
            pltpu.SemaphoreType.DMA((2,)),
        ],
    )
    return pl.pallas_call(
        _permute_kernel,
        grid_spec=grid_spec,
        out_shape=jax.ShapeDtypeStruct((MOE_NROWS, XS_W), F32),
        input_output_aliases={3: 0},
        compiler_params=_cparams("arbitrary"),
        name="moe_permute",
    )(dest, h2, wab, jnp.zeros((MOE_NROWS, XS_W), F32))


DMA_UNROLL = 8


def _start_row_copies(n_rows, copy_of_row):
    def body(g, carry):
        for u in range(DMA_UNROLL):
            copy_of_row(g * DMA_UNROLL + u).start(priority=u % 2)
        return carry

    lax.fori_loop(0, n_rows // DMA_UNROLL, body, 0)


def _start_row_gather(src_hbm, idx_ref, base, buf, sem, n_rows):
    _start_row_copies(n_rows, lambda r: pltpu.make_async_copy(
        src_hbm.at[pl.ds(idx_ref[base + r], 1)], buf.at[pl.ds(r, 1)], sem))


def _wait_row_gather(src_hbm, buf, sem, n_rows):
    pltpu.make_async_copy(src_hbm.at[pl.ds(0, n_rows)], buf, sem).wait()


def _moe_kernel(ea_ref, eb_ref, nused_ref, xs_ref, wga_ref, wua_ref, wda_ref, wgb_ref, wub_ref, wdb_ref, o_ref):
    i = pl.program_id(0)
    n_used = nused_ref[0]

    @pl.when(i < n_used)
    def _():
        xb = xs_ref[:, 0:D_MODEL].astype(BF16)

        def expert(wg_ref, wu_ref, wd_ref):
            g = jnp.dot(xb, wg_ref[0, 0], preferred_element_type=F32)
            u = jnp.dot(xb, wu_ref[0, 0], preferred_element_type=F32)
            act = (jax.nn.silu(g) * u).astype(BF16)
            return jnp.dot(act, wd_ref[0, 0], preferred_element_type=F32)

        wa = xs_ref[:, D_MODEL:D_MODEL + 1]
        wb = xs_ref[:, D_MODEL + 1:D_MODEL + 2]
        o_ref[...] = wa * expert(wga_ref, wua_ref, wda_ref) + wb * expert(wgb_ref, wub_ref, wdb_ref)

    @pl.when(i >= n_used)
    def _():
        o_ref[...] = jnp.zeros_like(o_ref)


def _moe(xs, block_ea, block_eb, n_used, wg_bf, wu_bf, wd_bf, l):
    up = lambda sel: pl.BlockSpec((1, 1, D_MODEL, D_FF_E), lambda i, ea, eb, nu: (l, (ea, eb)[sel][i], 0, 0))
    down = lambda sel: pl.BlockSpec((1, 1, D_FF_E, D_MODEL), lambda i, ea, eb, nu: (l, (ea, eb)[sel][i], 0, 0))
    grid_spec = pltpu.PrefetchScalarGridSpec(
        num_scalar_prefetch=3,
        grid=(MOE_BLOCKS,),
        in_specs=[
            pl.BlockSpec((MOE_ROWS, XS_W), lambda i, ea, eb, nu: (i, 0)),
            up(0), up(0), down(0), up(1), up(1), down(1),
        ],
        out_specs=pl.BlockSpec((MOE_ROWS, D_MODEL), lambda i, ea, eb, nu: (i, 0)),
    )
    return pl.pallas_call(
        _moe_kernel,
        grid_spec=grid_spec,
        out_shape=jax.ShapeDtypeStruct((MOE_NROWS, D_MODEL), F32),
        compiler_params=_cparams("arbitrary"),
        name="moe_experts",
    )(block_ea, block_eb, n_used, xs, wg_bf, wu_bf, wd_bf, wg_bf, wu_bf, wd_bf)


FIN_TM = 128
FIN_TILES = T_ALL // FIN_TM


def _final_kernel(pos_ref, y_hbm, x1_ref, mod_ref, g_ref, b_ref, o_ref, ybuf, sems):
    i = pl.program_id(0)
    slot = i % 2

    @pl.when(i == 0)
    def _():
        _start_row_gather(y_hbm, pos_ref, 0, ybuf.at[0], sems.at[0], FIN_TM)

    @pl.when(i + 1 < FIN_TILES)
    def _():
        _start_row_gather(y_hbm, pos_ref, (i + 1) * FIN_TM, ybuf.at[1 - slot], sems.at[1 - slot], FIN_TM)

    _wait_row_gather(y_hbm, ybuf.at[slot], sems.at[slot], FIN_TM)
    gate = mod_ref[0, 0, 5:6, :]
    o_ref[...] = _layer_norm_rows(ALPHA * x1_ref[...] + gate * ybuf[slot], g_ref[0, 1:2, :], b_ref[0, 1:2, :])


def _final(y_sorted, pos, x1, mods, ln_g, ln_b, l):
    per_group = TM // FIN_TM
    grid_spec = pltpu.PrefetchScalarGridSpec(
        num_scalar_prefetch=1,
        grid=(FIN_TILES,),
        in_specs=[
            pl.BlockSpec(memory_space=pl.ANY),
            pl.BlockSpec((FIN_TM, D_MODEL), lambda i, pos: (i, 0)),
            pl.BlockSpec((1, 1, 6, D_MODEL), lambda i, pos: (l, _tile_group(i // per_group), 0, 0)),
            pl.BlockSpec((1, 2, D_MODEL), lambda i, pos: (l, 0, 0)),
            pl.BlockSpec((1, 2, D_MODEL), lambda i, pos: (l, 0, 0)),
        ],
        out_specs=pl.BlockSpec((FIN_TM, D_MODEL), lambda i, pos: (i, 0)),
        scratch_shapes=[
            pltpu.VMEM((2, FIN_TM, D_MODEL), F32),
            pltpu.SemaphoreType.DMA((2,)),
        ],
    )
    return pl.pallas_call(
        _final_kernel,
        grid_spec=grid_spec,
        out_shape=jax.ShapeDtypeStruct((T_ALL, D_MODEL), F32),
        compiler_params=_cparams("arbitrary"),
        name="moe_combine_ln",
    )(pos, y_sorted, x1, mods, ln_g, ln_b)


def kernel(x_prompt, x_sample, c, cache_k, cache_v, state_hgrn, c_ctx, w_ada, b_ada, w_in, w_out, diff_lambda,
           attn_norm_g, mlp_ln_g, mlp_ln_b, w_spatial, b_spatial, hgrn_lb, hgrn_norm_g, ln_g, ln_b, w_router,
           router_bias, w_gate, w_up, w_down):
    x = jnp.concatenate([x_sample.reshape(T_DEC, D_MODEL), x_prompt.reshape(T_CTX, D_MODEL)], axis=0)
    cond = jnp.concatenate([c_ctx[None, :], c, jnp.zeros((COND_PAD - N_COND, D_MODEL), F32)], axis=0)
    mods = _ada_modulation(cond, w_ada, b_ada)

    w_in_bf, w_out_bf = w_in.astype(BF16), w_out.astype(BF16)
    wg_bf, wu_bf, wd_bf = w_gate.astype(BF16), w_up.astype(BF16), w_down.astype(BF16)
    cos_t, sin_t = _rope_tables()
    cache_k4 = cache_k.reshape(DEC_BATCH, DEPTH, PAST_LEN, QK_W_A)
    cache_v4 = cache_v.reshape(DEC_BATCH, DEPTH, PAST_LEN, W_A)
    b_spatial_t = jnp.swapaxes(b_spatial, 1, 2)
    w_router_t = w_router.T

    p = jax.nn.softmax(hgrn_lb.astype(F32), axis=1)
    cs = jnp.cumsum(p, axis=1)
    lb_all = (cs - cs[:, :1]).reshape(2, DEPTH, 1, WK_C)

    new_k, new_v, new_s = [], [], []
    for l in range(DEPTH):
        proj = _inproj(x, mods, w_in_bf, cos_t, sin_t, l)
        oa_dec = _attention(proj, cache_k4, cache_v4, diff_lambda, attn_norm_g, l, decode=True)
        oa_ctx = _attention(proj, None, None, diff_lambda, attn_norm_g, l, decode=False)
        o_b = _gmlp(proj, mlp_ln_g, mlp_ln_b, w_spatial, b_spatial_t, l)
        oc_dec, oc_ctx, s_ctx = _hgrn(proj, lb_all, state_hgrn[:, l], hgrn_norm_g, l)
        x1, h2, logits_t = _outproj(oa_dec, oa_ctx, o_b, oc_dec, oc_ctx, x, mods, w_out_bf, ln_g, ln_b,
                                    w_router_t, l)
        dest, wab, block_ea, block_eb, n_used = _route(logits_t, router_bias)
        xs = _permute(h2, dest, wab)
        y_sorted = _moe(xs, block_ea, block_eb, n_used, wg_bf, wu_bf, wd_bf, l)
        x = _final(y_sorted, dest, x1, mods, ln_g, ln_b, l)

        new_k.append(proj[T_DEC:, OFF_KA:OFF_KA + QK_W_A].reshape(BATCH, SEQ, H_A, 2, DQK_A))
        new_v.append(proj[T_DEC:, OFF_VA:OFF_VA + W_A].reshape(BATCH, SEQ, H_A, DV_A))
        new_s.append(s_ctx)

    return (x[T_DEC:].reshape(BATCH, SEQ, D_MODEL), x[:T_DEC].reshape(DEC_BATCH, DEC_SEQ, D_MODEL),
            jnp.stack(new_k, axis=1), jnp.stack(new_v, axis=1), jnp.stack(new_s, axis=1))
```
